```python
import jax, jax.numpy as jnp
from jax import lax
import numpy as np

D_MODEL = 1024
BATCH = 8
SEQ = 4096
DEPTH = 4

N_A_LAYERS = DEPTH // 2
N_B_LAYERS = DEPTH - N_A_LAYERS
HEAD_DIM = 64
MIX_WIDTH = D_MODEL
MAIN_WIDTH = 3 * MIX_WIDTH // 4
MEM_WIDTH = MIX_WIDTH // 4
CONV_CH = MAIN_WIDTH
CONV_K = 3
N_FOX_HEADS = MAIN_WIDTH // HEAD_DIM
FOX_WIDTH = N_FOX_HEADS * HEAD_DIM
N_MEM_HEADS = MEM_WIDTH // HEAD_DIM
N_MEM = 256
Q_BLOCK = 128
PEER_HEADS = 8
PEER_NKEYS = 128
PEER_N = PEER_NKEYS * PEER_NKEYS
PEER_DKEY = 256
PEER_HALF = PEER_DKEY // 2
PEER_TOPK = 16
PEER_CHUNK = 128
EPS = 1e-6

kernel_name = "yoco_shortconv_fox_memxattn_peer"


def rmsnorm(x, g):
    xf = x.astype(jnp.float32)
    y = xf * lax.rsqrt(jnp.mean(xf * xf, axis=-1, keepdims=True) + EPS)
    return (y * g.astype(jnp.float32)).astype(x.dtype)


def causal_dwconv(u, w):
    c = u.shape[-1]
    return lax.conv_general_dilated(
        u, w[:, None, :].astype(u.dtype), window_strides=(1,),
        padding=[(CONV_K - 1, 0)], dimension_numbers=("NWC", "WIO", "NWC"),
        feature_group_count=c)


def mem_attention(q, mk, mv):
    scale = HEAD_DIM ** -0.5
    logits = jnp.einsum("bshd,bmhd->bhsm", q, mk).astype(jnp.float32) * scale
    p = jax.nn.softmax(logits, axis=-1).astype(mv.dtype)
    return jnp.einsum("bhsm,bmhd->bshd", p, mv)


def fox_attention(q, k, v, logf):
    bsz, s_len, h, dh = q.shape
    scale = dh ** -0.5
    c = jnp.cumsum(logf, axis=1).transpose(0, 2, 1)
    nb = s_len // Q_BLOCK
    qb = q.reshape(bsz, nb, Q_BLOCK, h, dh).transpose(1, 0, 2, 3, 4)
    cb = c.reshape(bsz, h, nb, Q_BLOCK).transpose(2, 0, 1, 3)
    kpos = jnp.arange(s_len)

    def block(args):
        i, qi, ci = args
        logits = jnp.einsum("bqhd,bkhd->bhqk", qi, k).astype(jnp.float32) * scale
        logits = logits + ci[..., :, None] - c[:, :, None, :]
        qpos = i * Q_BLOCK + jnp.arange(Q_BLOCK)
        mask = kpos[None, :] <= qpos[:, None]
        logits = jnp.where(mask[None, None], logits, -jnp.inf)
        p = jax.nn.softmax(logits, axis=-1).astype(v.dtype)
        return jnp.einsum("bhqk,bkhd->bqhd", p, v)

    out = lax.map(block, (jnp.arange(nb), qb, cb))
    return out.transpose(1, 0, 2, 3, 4).reshape(bsz, s_len, h, dh)


def peer(xn, w_q, sub_keys, u_tab, v_tab):
    bsz, s_len, d = xn.shape
    t = bsz * s_len
    xt = xn.reshape(t, d)
    q = (xt @ w_q).reshape(t, PEER_HEADS, 2, PEER_HALF)
    s = jnp.einsum("thpd,hpnd->thpn", q, sub_keys).astype(jnp.float32)
    s1, i1 = lax.top_k(s[:, :, 0], PEER_TOPK)
    s2, i2 = lax.top_k(s[:, :, 1], PEER_TOPK)
    cand = (s1[..., :, None] + s2[..., None, :]).reshape(t, PEER_HEADS, PEER_TOPK * PEER_TOPK)
    cand_idx = (i1[..., :, None] * PEER_NKEYS + i2[..., None, :]).reshape(t, PEER_HEADS, PEER_TOPK * PEER_TOPK)
    top_s, pos = lax.top_k(cand, PEER_TOPK)
    idx = jnp.take_along_axis(cand_idx, pos, axis=-1)
    g = jax.nn.softmax(top_s, axis=-1).astype(xn.dtype)
    nc = t // PEER_CHUNK

    def chunk(args):
        xc, ic, gc = args
        u = jnp.take(u_tab, ic, axis=0)
        a = jnp.einsum("cd,chkd->chk", xc, u)
        w = gc * jax.nn.gelu(a, approximate=False)
        vv = jnp.take(v_tab, ic, axis=0)
        return jnp.einsum("chk,chkd->cd", w, vv)

    out = lax.map(chunk, (xt.reshape(nc, PEER_CHUNK, d),
                          idx.reshape(nc, PEER_CHUNK, PEER_HEADS, PEER_TOPK),
                          g.reshape(nc, PEER_CHUNK, PEER_HEADS, PEER_TOPK)))
    return out.reshape(bsz, s_len, d)


def setup_inputs(seed: int = 0) -> dict:
    key = jax.random.key(seed)
    ks = jax.random.split(key, 20)
    f32 = jnp.float32
    d = D_MODEL

    def nrm(k, shape, scale):
        return jax.random.normal(k, shape, f32) * scale

    def gain(k, shape):
        return 1.0 + 0.02 * jax.random.normal(k, shape, f32)

    return {
        "x": nrm(ks[0], (BATCH, SEQ, d), 1.0),
        "mem": nrm(ks[1], (BATCH, N_MEM, d), 1.0),
        "g_mix": gain(ks[2], (DEPTH, d)),
        "w_in_a": nrm(ks[3], (N_A_LAYERS, d, 3 * CONV_CH + MEM_WIDTH), d ** -0.5),
        "conv_w": nrm(ks[4], (N_A_LAYERS, CONV_K, CONV_CH), CONV_K ** -0.5),
        "w_in_b": nrm(ks[5], (N_B_LAYERS, d, FOX_WIDTH + MEM_WIDTH), d ** -0.5),
        "w_out": nrm(ks[6], (DEPTH, MIX_WIDTH, d), MIX_WIDTH ** -0.5),
        "g_mem": gain(ks[7], (DEPTH, d)),
        "w_mem_kv": nrm(ks[8], (DEPTH, d, 2 * MEM_WIDTH), d ** -0.5),
        "g_kv": gain(ks[9], (d,)),
        "w_kv": nrm(ks[10], (d, 2 * FOX_WIDTH + N_FOX_HEADS), d ** -0.5),
        "b_f": 2.0 + 0.1 * jax.random.normal(ks[11], (N_FOX_HEADS,), f32),
        "g_ffn": gain(ks[12], (DEPTH, d)),
        "peer_wq": nrm(ks[13], (DEPTH, d, PEER_HEADS * PEER_DKEY), d ** -0.5),
        "peer_keys": nrm(ks[14], (DEPTH, PEER_HEADS, 2, PEER_NKEYS, PEER_HALF), PEER_HALF ** -0.5),
        "peer_u": nrm(ks[15], (DEPTH, PEER_N, d), d ** -0.5),
        "peer_v": nrm(ks[16], (DEPTH, PEER_N, d), PEER_TOPK ** -0.5),
        "g_final": gain(ks[17], (d,)),
    }


def reference(x, mem, g_mix, w_in_a, conv_w, w_in_b, w_out, g_mem, w_mem_kv,
              g_kv, w_kv, b_f, g_ffn, peer_wq, peer_keys, peer_u, peer_v, g_final):
    bsz, s_len, _ = x.shape
    k_sh = v_sh = logf_sh = None
    for layer in range(DEPTH):
        if layer == N_A_LAYERS:
            hs = rmsnorm(x, g_kv)
            zkv = hs @ w_kv
            k_sh = zkv[..., :FOX_WIDTH].reshape(bsz, s_len, N_FOX_HEADS, HEAD_DIM)
            v_sh = zkv[..., FOX_WIDTH:2 * FOX_WIDTH].reshape(bsz, s_len, N_FOX_HEADS, HEAD_DIM)
            zf = zkv[..., 2 * FOX_WIDTH:].astype(jnp.float32) + b_f.astype(jnp.float32)
            logf_sh = jax.nn.log_sigmoid(zf)

        h = rmsnorm(x, g_mix[layer])
        mn = rmsnorm(mem, g_mem[layer])
        mkv = (mn @ w_mem_kv[layer]).reshape(bsz, N_MEM, 2, N_MEM_HEADS, HEAD_DIM)
        mk, mv = mkv[:, :, 0], mkv[:, :, 1]

        if layer < N_A_LAYERS:
            z = h @ w_in_a[layer]
            u = z[..., :CONV_CH]
            c_gate = z[..., CONV_CH:2 * CONV_CH]
            b_gate = z[..., 2 * CONV_CH:3 * CONV_CH]
            qm = z[..., 3 * CONV_CH:]
            y_main = b_gate * causal_dwconv(c_gate * u, conv_w[layer])
        else:
            z = h @ w_in_b[layer - N_A_LAYERS]
            qf = z[..., :FOX_WIDTH].reshape(bsz, s_len, N_FOX_HEADS, HEAD_DIM)
            qm = z[..., FOX_WIDTH:]
            y_main = fox_attention(qf, k_sh, v_sh, logf_sh).reshape(bsz, s_len, FOX_WIDTH)

        y_mem = mem_attention(qm.reshape(bsz, s_len, N_MEM_HEADS, HEAD_DIM), mk, mv)
        y_mem = y_mem.reshape(bsz, s_len, MEM_WIDTH)
        x = x + jnp.concatenate([y_main, y_mem], axis=-1) @ w_out[layer]
        x = x + peer(rmsnorm(x, g_ffn[layer]), peer_wq[layer], peer_keys[layer],
                     peer_u[layer], peer_v[layer])
    return rmsnorm(x, g_final)
```

```python
import functools

import jax
import jax.numpy as jnp
from jax import lax
from jax.experimental import pallas as pl
from jax.experimental.pallas import tpu as pltpu

EPS = 1e-6
HEAD_DIM = 64
CONV_K = 3
PEER_HEADS = 8
PEER_NKEYS = 128
PEER_HALF = 128
PEER_TOPK = 16
PEER_CHUNK = 128
LANES = 128
SUBLANES = 8
TOKEN_TILE = 512
ROUTE_TILE = 256
ATTN_TILE = 512
VMEM_LIMIT = 56 * 1024 * 1024

BF16 = jnp.bfloat16
F32 = jnp.float32


def _params(*sem):
    return pltpu.CompilerParams(dimension_semantics=sem, vmem_limit_bytes=VMEM_LIMIT)


def _rms(x, g):
    return x * lax.rsqrt(jnp.mean(x * x, axis=-1, keepdims=True) + EPS) * g


def _mem_attend(qm, mkbd, mvbd, n_heads, n_mem):
    lg = jnp.dot(qm.astype(BF16), mkbd, preferred_element_type=F32) * (HEAD_DIM ** -0.5)
    parts = []
    for h in range(n_heads):
        l = lg[:, h * n_mem:(h + 1) * n_mem]
        e = jnp.exp(l - jnp.max(l, axis=-1, keepdims=True))
        parts.append((e / jnp.sum(e, axis=-1, keepdims=True)).astype(BF16))
    return jnp.dot(jnp.concatenate(parts, axis=1), mvbd, preferred_element_type=F32)


def _memkv_body(mem_ref, g_ref, w_ref, o_ref):
    mn = _rms(mem_ref[0], g_ref[0])
    o_ref[0, 0] = jnp.dot(mn.astype(BF16), w_ref[0], preferred_element_type=F32).astype(BF16)


def _memkv(mem, g_mem, w_mem_kv):
    bsz, n_mem, d = mem.shape
    depth, _, wd = w_mem_kv.shape
    return pl.pallas_call(
        _memkv_body,
        grid=(depth, bsz),
        in_specs=[
            pl.BlockSpec((1, n_mem, d), lambda l, b: (b, 0, 0)),
            pl.BlockSpec((1, 1, d), lambda l, b: (l, 0, 0)),
            pl.BlockSpec((1, d, wd), lambda l, b: (l, 0, 0)),
        ],
        out_specs=pl.BlockSpec((1, 1, n_mem, wd), lambda l, b: (l, b, 0, 0)),
        out_shape=jax.ShapeDtypeStruct((depth, bsz, n_mem, wd), BF16),
        compiler_params=_params("arbitrary", "arbitrary"),
        name="memkv",
    )(mem, g_mem.reshape(depth, 1, d), w_mem_kv.astype(BF16))


def _block_diag_mem(mkv, n_heads):
    depth, bsz, n_mem, _ = mkv.shape
    r = mkv.reshape(depth, bsz, n_mem, 2, n_heads, HEAD_DIM)
    eye = jnp.eye(n_heads, dtype=mkv.dtype)
    mk = jnp.einsum("lbmhd,hg->lbhdgm", r[:, :, :, 0], eye)
    mv = jnp.einsum("lbmhd,hg->lbhmgd", r[:, :, :, 1], eye)
    hw = n_heads * HEAD_DIM
    return mk.reshape(depth, bsz, hw, n_heads * n_mem), mv.reshape(depth, bsz, n_heads * n_mem, hw)


def _mix_a_body(has_po, conv_ch, n_mem_heads, n_mem, *refs):
    if has_po:
        x_ref, po_ref, g_ref, win_ref, cw_ref, mk_ref, mv_ref, wout_ref, o_ref, cbuf = refs
        x = x_ref[0] + po_ref[0]
    else:
        x_ref, g_ref, win_ref, cw_ref, mk_ref, mv_ref, wout_ref, o_ref, cbuf = refs
        x = x_ref[0]
    tm = x.shape[0]
    h = _rms(x, g_ref[...]).astype(BF16)
    z = jnp.dot(h, win_ref[...], preferred_element_type=F32)
    u = z[:, :conv_ch]
    c_gate = z[:, conv_ch:2 * conv_ch]
    b_gate = z[:, 2 * conv_ch:3 * conv_ch]
    qm = z[:, 3 * conv_ch:]
    cu = c_gate * u

    @pl.when(pl.program_id(1) == 0)
    def _():
        cbuf[0:SUBLANES, :] = jnp.zeros((SUBLANES, conv_ch), F32)

    cbuf[SUBLANES:SUBLANES + tm, :] = cu
    cu1 = cbuf[SUBLANES - 1:SUBLANES - 1 + tm, :]
    cu2 = cbuf[SUBLANES - 2:SUBLANES - 2 + tm, :]
    cw = cw_ref[...]
    y = b_gate * (cw[0:1] * cu2 + cw[1:2] * cu1 + cw[2:3] * cu)
    cbuf[0:SUBLANES, :] = cbuf[tm:tm + SUBLANES, :]

    ymem = _mem_attend(qm, mk_ref[0], mv_ref[0], n_mem_heads, n_mem)
    o_ref[0] = (x
                + jnp.dot(y.astype(BF16), wout_ref[0:conv_ch, :], preferred_element_type=F32)
                + jnp.dot(ymem.astype(BF16), wout_ref[conv_ch:, :], preferred_element_type=F32))


def _mix_a(x, po, g, w_in, conv_w, mkbd, mvbd, w_out):
    bsz, s_len, d = x.shape
    tm = min(TOKEN_TILE, s_len)
    conv_ch = conv_w.shape[1]
    mem_w = w_in.shape[1] - 3 * conv_ch
    n_mem_heads = mem_w // HEAD_DIM
    n_mem = mkbd.shape[2] // n_mem_heads
    tok = pl.BlockSpec((1, tm, d), lambda b, i: (b, i, 0))
    full = lambda a: pl.BlockSpec(a.shape, lambda b, i: (0,) * a.ndim)
    per_b = lambda a: pl.BlockSpec((1,) + a.shape[1:], lambda b, i: (b, 0, 0))
    g2 = g.reshape(1, d)
    win = w_in.astype(BF16)
    wout = w_out.astype(BF16)
    args = [x] + ([po] if po is not None else []) + [g2, win, conv_w, mkbd, mvbd, wout]
    specs = [tok] + ([tok] if po is not None else []) + [full(g2), full(win), full(conv_w), per_b(mkbd), per_b(mvbd), full(wout)]
    return pl.pallas_call(
        functools.partial(_mix_a_body, po is not None, conv_ch, n_mem_heads, n_mem),
        grid=(bsz, s_len // tm),
        in_specs=specs,
        out_specs=tok,
        out_shape=jax.ShapeDtypeStruct(x.shape, F32),
        scratch_shapes=[pltpu.VMEM((tm + SUBLANES, conv_ch), F32)],
        compiler_params=_params("arbitrary", "arbitrary"),
        name="mix_a",
    )(*args)


def _topk_rows(s, k, payload=None):
    n = s.shape[0]
    rows = lax.broadcasted_iota(jnp.int32, s.shape, 0)
    vals, outs = [], []
    for _ in range(k):
        m = jnp.max(s, axis=0, keepdims=True)
        pos = jnp.min(jnp.where(s == m, rows, n), axis=0, keepdims=True)
        sel = rows == pos
        vals.append(m)
        if payload is None:
            outs.append(pos)
        else:
            outs.append(jnp.sum(jnp.where(sel, payload, 0), axis=0, keepdims=True))
        s = jnp.where(sel, -jnp.inf, s)
    return jnp.concatenate(vals, axis=0), jnp.concatenate(outs, axis=0)


def _route_body(x_ref, g_ref, wq_ref, keys_ref, xn_ref, idx_ref, gate_ref, xnb):
    @pl.when(pl.program_id(1) == 0)
    def _():
        xn = _rms(x_ref[...], g_ref[...])
        xn_ref[...] = xn
        xnb[...] = xn.astype(BF16)

    q = jnp.dot(xnb[...], wq_ref[...], preferred_element_type=F32)
    tops = []
    for p in range(2):
        qp = q[:, p * PEER_HALF:(p + 1) * PEER_HALF].astype(BF16)
        st = lax.dot_general(keys_ref[0, p], qp, (((1,), (1,)), ((), ())),
                             preferred_element_type=F32)
        tops.append(_topk_rows(st, PEER_TOPK))
    (v1, i1), (v2, i2) = tops
    cand = jnp.concatenate([v1[i:i + 1] + v2 for i in range(PEER_TOPK)], axis=0)
    cidx = jnp.concatenate([i1[i:i + 1] * PEER_NKEYS + i2 for i in range(PEER_TOPK)], axis=0)
    top_s, idx = _topk_rows(cand, PEER_TOPK, payload=cidx)
    e = jnp.exp(top_s - top_s[0:1])
    idx_ref[...] = idx
    gate_ref[...] = e / jnp.sum(e, axis=0, keepdims=True)


def _peer_route(x2, g, w_q, keys):
    t, d = x2.shape
    tm = min(ROUTE_TILE, t)
    dk = 2 * PEER_HALF
    return pl.pallas_call(
        _route_body,
        grid=(t // tm, PEER_HEADS),
        in_specs=[
            pl.BlockSpec((tm, d), lambda i, h: (i, 0)),
            pl.BlockSpec((1, d), lambda i, h: (0, 0)),
            pl.BlockSpec((d, dk), lambda i, h: (0, h)),
            pl.BlockSpec((1, 2, PEER_NKEYS, PEER_HALF), lambda i, h: (h, 0, 0, 0)),
        ],
        out_specs=[
            pl.BlockSpec((tm, d), lambda i, h: (i, 0)),
            pl.BlockSpec((PEER_TOPK, tm), lambda i, h: (h, i)),
            pl.BlockSpec((PEER_TOPK, tm), lambda i, h: (h, i)),
        ],
        out_shape=[
            jax.ShapeDtypeStruct((t, d), F32),
            jax.ShapeDtypeStruct((PEER_HEADS * PEER_TOPK, t), jnp.int32),
            jax.ShapeDtypeStruct((PEER_HEADS * PEER_TOPK, t), F32),
        ],
        scratch_shapes=[pltpu.VMEM((tm, d), BF16)],
        compiler_params=_params("arbitrary", "arbitrary"),
        name="peer_route",
    )(x2, g.reshape(1, d), w_q.astype(BF16), keys.astype(BF16))


def _peer_retrieve(xn, idx, gate, u_tab, v_tab):
    t, d = xn.shape
    nc = t // PEER_CHUNK

    def chunk(args):
        xc, ic, gc = args
        u = jnp.take(u_tab, ic, axis=0)
        a = jnp.einsum("cd,ckd->ck", xc, u)
        w = gc * jax.nn.gelu(a, approximate=False)
        vv = jnp.take(v_tab, ic, axis=0)
        return jnp.einsum("ck,ckd->cd", w, vv)

    hk = idx.shape[1]
    out = lax.map(chunk, (xn.reshape(nc, PEER_CHUNK, d), idx.reshape(nc, PEER_CHUNK, hk),
                          gate.reshape(nc, PEER_CHUNK, hk)))
    return out.reshape(t, d)


def _peer(x, g, w_q, keys, u_tab, v_tab):
    bsz, s_len, d = x.shape
    xn, idx_t, gate_t = _peer_route(x.reshape(bsz * s_len, d), g, w_q, keys)
    return _peer_retrieve(xn, idx_t.T, gate_t.T, u_tab, v_tab).reshape(bsz, s_len, d)


def _kv_body(fox_w, x_ref, po_ref, g_ref, w_ref, bf_ref, k_ref, v_ref, c_ref, ct_ref, carry):
    x = x_ref[0] + po_ref[0]
    tm = x.shape[0]
    hs = _rms(x, g_ref[...]).astype(BF16)
    z = jnp.dot(hs, w_ref[...], preferred_element_type=F32)
    k_ref[0] = z[:, :fox_w].astype(BF16)
    v_ref[0] = z[:, fox_w:2 * fox_w].astype(BF16)
    zf = z[:, 2 * fox_w:] + bf_ref[...]
    logf = jnp.minimum(zf, 0.0) - jnp.log1p(jnp.exp(-jnp.abs(zf)))

    @pl.when(pl.program_id(1) == 0)
    def _():
        carry[...] = jnp.zeros(carry.shape, F32)

    r = lax.broadcasted_iota(jnp.int32, (tm, tm), 0)
    c = lax.broadcasted_iota(jnp.int32, (tm, tm), 1)
    tri = (c <= r).astype(F32)
    cs = jnp.dot(tri, logf, preferred_element_type=F32, precision=lax.Precision.HIGHEST) + carry[...]
    carry[...] = cs[tm - 1:tm, :]
    c_ref[0] = cs
    ct_ref[0] = cs.T[0:ct_ref.shape[1], :]


def _kv_proj(x, po, g, w_kv, b_f):
    bsz, s_len, d = x.shape
    tm = min(TOKEN_TILE, s_len)
    n_heads = b_f.shape[0]
    fox_w = (w_kv.shape[1] - n_heads) // 2
    w = jnp.pad(w_kv, ((0, 0), (0, LANES - n_heads))).astype(BF16)
    bf = jnp.pad(b_f, (0, LANES - n_heads)).reshape(1, LANES)
    hp = -(-n_heads // SUBLANES) * SUBLANES
    tok = lambda w_, dt: (pl.BlockSpec((1, tm, w_), lambda b, i: (b, i, 0)), jax.ShapeDtypeStruct((bsz, s_len, w_), dt))
    (ks, ksh), (vs, vsh), (cs, csh) = tok(fox_w, BF16), tok(fox_w, BF16), tok(LANES, F32)
    return pl.pallas_call(
        functools.partial(_kv_body, fox_w),
        grid=(bsz, s_len // tm),
        in_specs=[
            pl.BlockSpec((1, tm, d), lambda b, i: (b, i, 0)),
            pl.BlockSpec((1, tm, d), lambda b, i: (b, i, 0)),
            pl.BlockSpec((1, d), lambda b, i: (0, 0)),
            pl.BlockSpec(w.shape, lambda b, i: (0, 0)),
            pl.BlockSpec((1, LANES), lambda b, i: (0, 0)),
        ],
        out_specs=[ks, vs, cs, pl.BlockSpec((1, hp, tm), lambda b, i: (b, 0, i))],
        out_shape=[ksh, vsh, csh, jax.ShapeDtypeStruct((bsz, hp, s_len), F32)],
        scratch_shapes=[pltpu.VMEM((1, LANES), F32)],
        compiler_params=_params("arbitrary", "arbitrary"),
        name="kv_proj",
    )(x, po, g.reshape(1, d), w, bf)


def _mix_b_in_body(fox_w, n_mem_heads, n_mem, x_ref, po_ref, g_ref, win_ref, mk_ref, mv_ref, q_ref, ym_ref):
    x = x_ref[0] + po_ref[0]
    h = _rms(x, g_ref[...]).astype(BF16)
    z = jnp.dot(h, win_ref[...], preferred_element_type=F32)
    q_ref[0] = (z[:, :fox_w] * (HEAD_DIM ** -0.5)).astype(BF16)
    ym_ref[0] = _mem_attend(z[:, fox_w:], mk_ref[0], mv_ref[0], n_mem_heads, n_mem).astype(BF16)


def _mix_b_in(x, po, g, w_in, mkbd, mvbd):
    bsz, s_len, d = x.shape
    tm = min(TOKEN_TILE, s_len)
    mem_w = mvbd.shape[2]
    fox_w = w_in.shape[1] - mem_w
    n_mem_heads = mem_w // HEAD_DIM
    n_mem = mkbd.shape[2] // n_mem_heads
    win = w_in.astype(BF16)
    tok = lambda w_: pl.BlockSpec((1, tm, w_), lambda b, i: (b, i, 0))
    per_b = lambda a: pl.BlockSpec((1,) + a.shape[1:], lambda b, i: (b, 0, 0))
    return pl.pallas_call(
        functools.partial(_mix_b_in_body, fox_w, n_mem_heads, n_mem),
        grid=(bsz, s_len // tm),
        in_specs=[tok(d), tok(d), pl.BlockSpec((1, d), lambda b, i: (0, 0)),
                  pl.BlockSpec(win.shape, lambda b, i: (0, 0)), per_b(mkbd), per_b(mvbd)],
        out_specs=[tok(fox_w), tok(mem_w)],
        out_shape=[jax.ShapeDtypeStruct((bsz, s_len, fox_w), BF16),
                   jax.ShapeDtypeStruct((bsz, s_len, mem_w), BF16)],
        compiler_params=_params("arbitrary", "arbitrary"),
        name="mix_b_in",
    )(x, po, g.reshape(1, d), win, mkbd, mvbd)


def _fox_body(n_heads, q_ref, k_ref, v_ref, c_ref, ct_ref, o_ref, m_scr, l_scr, acc):
    qi, ki = pl.program_id(1), pl.program_id(2)
    tq, tk = q_ref.shape[1], k_ref.shape[1]

    @pl.when(ki == 0)
    def _():
        m_scr[...] = jnp.full(m_scr.shape, -jnp.inf, F32)
        l_scr[...] = jnp.zeros(l_scr.shape, F32)
        acc[...] = jnp.zeros(acc.shape, F32)

    def step(masked):
        if masked:
            keep = (lax.broadcasted_iota(jnp.int32, (tq, tk), 0)
                    >= lax.broadcasted_iota(jnp.int32, (tq, tk), 1))
        for h in range(n_heads):
            sl = slice(h * HEAD_DIM, (h + 1) * HEAD_DIM)
            s = lax.dot_general(q_ref[0, :, sl], k_ref[0, :, sl], (((1,), (1,)), ((), ())),
                                preferred_element_type=F32)
            s = s + (c_ref[0, :, h:h + 1] - ct_ref[0, h:h + 1, :])
            if masked:
                s = jnp.where(keep, s, -jnp.inf)
            m_old = m_scr[:, h:h + 1]
            m_new = jnp.maximum(m_old, jnp.max(s, axis=-1, keepdims=True))
            alpha = jnp.exp(m_old - m_new)
            p = jnp.exp(s - m_new)
            l_scr[:, h:h + 1] = alpha * l_scr[:, h:h + 1] + jnp.sum(p, axis=-1, keepdims=True)
            acc[:, sl] = alpha * acc[:, sl] + jnp.dot(p.astype(BF16), v_ref[0, :, sl],
                                                      preferred_element_type=F32)
            m_scr[:, h:h + 1] = m_new

    @pl.when(ki < qi)
    def _():
        step(False)

    @pl.when(ki == qi)
    def _():
        step(True)
        for h in range(n_heads):
            sl = slice(h * HEAD_DIM, (h + 1) * HEAD_DIM)
            o_ref[0, :, sl] = (acc[:, sl] / l_scr[:, h:h + 1]).astype(BF16)


def _fox(q, k, v, c, ct):
    bsz, s_len, w = q.shape
    n_heads = w // HEAD_DIM
    t = min(ATTN_TILE, s_len)
    nb = s_len // t
    return pl.pallas_call(
        functools.partial(_fox_body, n_heads),
        grid=(bsz, nb, nb),
        in_specs=[
            pl.BlockSpec((1, t, w), lambda b, i, j: (b, i, 0)),
            pl.BlockSpec((1, t, w), lambda b, i, j: (b, jnp.minimum(i, j), 0)),
            pl.BlockSpec((1, t, w), lambda b, i, j: (b, jnp.minimum(i, j), 0)),
            pl.BlockSpec((1, t, c.shape[2]), lambda b, i, j: (b, i, 0)),
            pl.BlockSpec((1, ct.shape[1], t), lambda b, i, j: (b, 0, jnp.minimum(i, j))),
        ],
        out_specs=pl.BlockSpec((1, t, w), lambda b, i, j: (b, i, 0)),
        out_shape=jax.ShapeDtypeStruct((bsz, s_len, w), BF16),
        scratch_shapes=[pltpu.VMEM((t, LANES), F32), pltpu.VMEM((t, LANES), F32), pltpu.VMEM((t, w), F32)],
        compiler_params=_params("arbitrary", "arbitrary", "arbitrary"),
        name="fox",
    )(q, k, v, c, ct)


def _mix_b_out_body(fox_w, x_ref, po_ref, y_ref, ym_ref, wout_ref, o_ref):
    o_ref[0] = (x_ref[0] + po_ref[0]
                + jnp.dot(y_ref[0], wout_ref[0:fox_w, :], preferred_element_type=F32)
                + jnp.dot(ym_ref[0], wout_ref[fox_w:, :], preferred_element_type=F32))


def _mix_b_out(x, po, y, ym, w_out):
    bsz, s_len, d = x.shape
    tm = min(TOKEN_TILE, s_len)
    fox_w = y.shape[2]
    wout = w_out.astype(BF16)
    tok = lambda w_: pl.BlockSpec((1, tm, w_), lambda b, i: (b, i, 0))
    return pl.pallas_call(
        functools.partial(_mix_b_out_body, fox_w),
        grid=(bsz, s_len // tm),
        in_specs=[tok(d), tok(d), tok(fox_w), tok(ym.shape[2]), pl.BlockSpec(wout.shape, lambda b, i: (0, 0))],
        out_specs=tok(d),
        out_shape=jax.ShapeDtypeStruct(x.shape, F32),
        compiler_params=_params("arbitrary", "arbitrary"),
        name="mix_b_out",
    )(x, po, y, ym, wout)


def _final_body(x_ref, po_ref, g_ref, o_ref):
    o_ref[0] = _rms(x_ref[0] + po_ref[0], g_ref[...])


def _final_norm(x, po, g):
    bsz, s_len, d = x.shape
    tm = min(TOKEN_TILE, s_len)
    tok = pl.BlockSpec((1, tm, d), lambda b, i: (b, i, 0))
    return pl.pallas_call(
        _final_body,
        grid=(bsz, s_len // tm),
        in_specs=[tok, tok, pl.BlockSpec((1, d), lambda b, i: (0, 0))],
        out_specs=tok,
        out_shape=jax.ShapeDtypeStruct(x.shape, F32),
        compiler_params=_params("arbitrary", "arbitrary"),
        name="final_norm",
    )(x, po, g.reshape(1, d))


def kernel(x, mem, g_mix, w_in_a, conv_w, w_in_b, w_out, g_mem, w_mem_kv, g_kv, w_kv, b_f, g_ffn,
           peer_wq, peer_keys, peer_u, peer_v, g_final):
    depth = g_mix.shape[0]
    n_a = w_in_a.shape[0]
    n_mem_heads = w_mem_kv.shape[2] // (2 * HEAD_DIM)
    mkbd, mvbd = _block_diag_mem(_memkv(mem, g_mem, w_mem_kv), n_mem_heads)

    po = None
    k_sh = v_sh = c_sh = ct_sh = None
    for layer in range(depth):
        if layer == n_a:
            k_sh, v_sh, c_sh, ct_sh = _kv_proj(x, po, g_kv, w_kv, b_f)
        if layer < n_a:
            x = _mix_a(x, po, g_mix[layer], w_in_a[layer], conv_w[layer], mkbd[layer], mvbd[layer], w_out[layer])
        else:
            q, ym = _mix_b_in(x, po, g_mix[layer], w_in_b[layer - n_a], mkbd[layer], mvbd[layer])
            y = _fox(q, k_sh, v_sh, c_sh, ct_sh)
            x = _mix_b_out(x, po, y, ym, w_out[layer])
        po = _peer(x, g_ffn[layer], peer_wq[layer], peer_keys[layer], peer_u[layer], peer_v[layer])
    return _final_norm(x, po, g_final)
```

```python
import functools

import jax
import jax.numpy as jnp
from jax import lax
from jax.experimental import pallas as pl
from jax.experimental.pallas import tpu as pltpu
from jax.experimental.pallas import tpu_sc as plsc

EPS = 1e-6
HEAD_DIM = 64
CONV_K = 3
PEER_HEADS = 8
PEER_NKEYS = 128
PEER_HALF = 128
PEER_TOPK = 16
PEER_CHUNK = 128
LANES = 128
SUBLANES = 8
TOKEN_TILE = 512
ROUTE_TILE = 256
ATTN_TILE = 512
VMEM_LIMIT = 56 * 1024 * 1024
SC_CORES = 2
SC_SUBCORES = 16
SC_LANES = 16
SC_ROWS = 32
SC_GROUP = 8

BF16 = jnp.bfloat16
F32 = jnp.float32


def _params(*sem):
    return pltpu.CompilerParams(dimension_semantics=sem, vmem_limit_bytes=VMEM_LIMIT)


def _rms(x, g):
    return x * lax.rsqrt(jnp.mean(x * x, axis=-1, keepdims=True) + EPS) * g


def _mem_attend(qm, mkbd, mvbd, n_heads, n_mem):
    lg = jnp.dot(qm.astype(BF16), mkbd, preferred_element_type=F32) * (HEAD_DIM ** -0.5)
    parts = []
    for h in range(n_heads):
        l = lg[:, h * n_mem:(h + 1) * n_mem]
        e = jnp.exp(l - jnp.max(l, axis=-1, keepdims=True))
        parts.append((e / jnp.sum(e, axis=-1, keepdims=True)).astype(BF16))
    return jnp.dot(jnp.concatenate(parts, axis=1), mvbd, preferred_element_type=F32)


def _memkv_body(mem_ref, g_ref, w_ref, o_ref):
    mn = _rms(mem_ref[0], g_ref[0])
    o_ref[0, 0] = jnp.dot(mn.astype(BF16), w_ref[0], preferred_element_type=F32).astype(BF16)


def _memkv(mem, g_mem, w_mem_kv):
    bsz, n_mem, d = mem.shape
    depth, _, wd = w_mem_kv.shape
    return pl.pallas_call(
        _memkv_body,
        grid=(depth, bsz),
        in_specs=[
            pl.BlockSpec((1, n_mem, d), lambda l, b: (b, 0, 0)),
            pl.BlockSpec((1, 1, d), lambda l, b: (l, 0, 0)),
            pl.BlockSpec((1, d, wd), lambda l, b: (l, 0, 0)),
        ],
        out_specs=pl.BlockSpec((1, 1, n_mem, wd), lambda l, b: (l, b, 0, 0)),
        out_shape=jax.ShapeDtypeStruct((depth, bsz, n_mem, wd), BF16),
        compiler_params=_params("arbitrary", "arbitrary"),
        name="memkv",
    )(mem, g_mem.reshape(depth, 1, d), w_mem_kv.astype(BF16))


def _block_diag_mem(mkv, n_heads):
    depth, bsz, n_mem, _ = mkv.shape
    r = mkv.reshape(depth, bsz, n_mem, 2, n_heads, HEAD_DIM)
    eye = jnp.eye(n_heads, dtype=mkv.dtype)
    mk = jnp.einsum("lbmhd,hg->lbhdgm", r[:, :, :, 0], eye)
    mv = jnp.einsum("lbmhd,hg->lbhmgd", r[:, :, :, 1], eye)
    hw = n_heads * HEAD_DIM
    return mk.reshape(depth, bsz, hw, n_heads * n_mem), mv.reshape(depth, bsz, n_heads * n_mem, hw)


def _mix_a_body(has_po, conv_ch, n_mem_heads, n_mem, *refs):
    if has_po:
        x_ref, po_ref, g_ref, win_ref, cw_ref, mk_ref, mv_ref, wout_ref, o_ref, cbuf = refs
        x = x_ref[0] + po_ref[0]
    else:
        x_ref, g_ref, win_ref, cw_ref, mk_ref, mv_ref, wout_ref, o_ref, cbuf = refs
        x = x_ref[0]
    tm = x.shape[0]
    h = _rms(x, g_ref[...]).astype(BF16)
    z = jnp.dot(h, win_ref[...], preferred_element_type=F32)
    u = z[:, :conv_ch]
    c_gate = z[:, conv_ch:2 * conv_ch]
    b_gate = z[:, 2 * conv_ch:3 * conv_ch]
    qm = z[:, 3 * conv_ch:]
    cu = c_gate * u

    @pl.when(pl.program_id(1) == 0)
    def _():
        cbuf[0:SUBLANES, :] = jnp.zeros((SUBLANES, conv_ch), F32)

    cbuf[SUBLANES:SUBLANES + tm, :] = cu
    cu1 = cbuf[SUBLANES - 1:SUBLANES - 1 + tm, :]
    cu2 = cbuf[SUBLANES - 2:SUBLANES - 2 + tm, :]
    cw = cw_ref[...]
    y = b_gate * (cw[0:1] * cu2 + cw[1:2] * cu1 + cw[2:3] * cu)
    cbuf[0:SUBLANES, :] = cbuf[tm:tm + SUBLANES, :]

    ymem = _mem_attend(qm, mk_ref[0], mv_ref[0], n_mem_heads, n_mem)
    o_ref[0] = (x
                + jnp.dot(y.astype(BF16), wout_ref[0:conv_ch, :], preferred_element_type=F32)
                + jnp.dot(ymem.astype(BF16), wout_ref[conv_ch:, :], preferred_element_type=F32))


def _mix_a(x, po, g, w_in, conv_w, mkbd, mvbd, w_out):
    bsz, s_len, d = x.shape
    tm = min(TOKEN_TILE, s_len)
    conv_ch = conv_w.shape[1]
    mem_w = w_in.shape[1] - 3 * conv_ch
    n_mem_heads = mem_w // HEAD_DIM
    n_mem = mkbd.shape[2] // n_mem_heads
    tok = pl.BlockSpec((1, tm, d), lambda b, i: (b, i, 0))
    full = lambda a: pl.BlockSpec(a.shape, lambda b, i: (0,) * a.ndim)
    per_b = lambda a: pl.BlockSpec((1,) + a.shape[1:], lambda b, i: (b, 0, 0))
    g2 = g.reshape(1, d)
    win = w_in.astype(BF16)
    wout = w_out.astype(BF16)
    args = [x] + ([po] if po is not None else []) + [g2, win, conv_w, mkbd, mvbd, wout]
    specs = [tok] + ([tok] if po is not None else []) + [full(g2), full(win), full(conv_w), per_b(mkbd), per_b(mvbd), full(wout)]
    return pl.pallas_call(
        functools.partial(_mix_a_body, po is not None, conv_ch, n_mem_heads, n_mem),
        grid=(bsz, s_len // tm),
        in_specs=specs,
        out_specs=tok,
        out_shape=jax.ShapeDtypeStruct(x.shape, F32),
        scratch_shapes=[pltpu.VMEM((tm + SUBLANES, conv_ch), F32)],
        compiler_params=_params("arbitrary", "arbitrary"),
        name="mix_a",
    )(*args)


def _topk_rows(s, k, payload=None):
    n = s.shape[0]
    rows = lax.broadcasted_iota(jnp.int32, s.shape, 0)
    vals, outs = [], []
    for _ in range(k):
        m = jnp.max(s, axis=0, keepdims=True)
        pos = jnp.min(jnp.where(s == m, rows, n), axis=0, keepdims=True)
        sel = rows == pos
        vals.append(m)
        if payload is None:
            outs.append(pos)
        else:
            outs.append(jnp.sum(jnp.where(sel, payload, 0), axis=0, keepdims=True))
        s = jnp.where(sel, -jnp.inf, s)
    return jnp.concatenate(vals, axis=0), jnp.concatenate(outs, axis=0)


def _route_body(x_ref, g_ref, wq_ref, keys_ref, xn_ref, idx_ref, gate_ref, xnb):
    @pl.when(pl.program_id(1) == 0)
    def _():
        xn = _rms(x_ref[...], g_ref[...])
        xn_ref[...] = xn
        xnb[...] = xn.astype(BF16)

    q = jnp.dot(xnb[...], wq_ref[...], preferred_element_type=F32)
    tops = []
    for p in range(2):
        qp = q[:, p * PEER_HALF:(p + 1) * PEER_HALF].astype(BF16)
        st = lax.dot_general(keys_ref[0, p], qp, (((1,), (1,)), ((), ())),
                             preferred_element_type=F32)
        tops.append(_topk_rows(st, PEER_TOPK))
    (v1, i1), (v2, i2) = tops
    cand = jnp.concatenate([v1[i:i + 1] + v2 for i in range(PEER_TOPK)], axis=0)
    cidx = jnp.concatenate([i1[i:i + 1] * PEER_NKEYS + i2 for i in range(PEER_TOPK)], axis=0)
    top_s, idx = _topk_rows(cand, PEER_TOPK, payload=cidx)
    e = jnp.exp(top_s - top_s[0:1])
    idx_ref[...] = idx
    gate_ref[...] = e / jnp.sum(e, axis=0, keepdims=True)


def _peer_route(x2, g, w_q, keys):
    t, d = x2.shape
    tm = min(ROUTE_TILE, t)
    dk = 2 * PEER_HALF
    return pl.pallas_call(
        _route_body,
        grid=(t // tm, PEER_HEADS),
        in_specs=[
            pl.BlockSpec((tm, d), lambda i, h: (i, 0)),
            pl.BlockSpec((1, d), lambda i, h: (0, 0)),
            pl.BlockSpec((d, dk), lambda i, h: (0, h)),
            pl.BlockSpec((1, 2, PEER_NKEYS, PEER_HALF), lambda i, h: (h, 0, 0, 0)),
        ],
        out_specs=[
            pl.BlockSpec((tm, d), lambda i, h: (i, 0)),
            pl.BlockSpec((PEER_TOPK, tm), lambda i, h: (h, i)),
            pl.BlockSpec((PEER_TOPK, tm), lambda i, h: (h, i)),
        ],
        out_shape=[
            jax.ShapeDtypeStruct((t, d), F32),
            jax.ShapeDtypeStruct((PEER_HEADS * PEER_TOPK, t), jnp.int32),
            jax.ShapeDtypeStruct((PEER_HEADS * PEER_TOPK, t), F32),
        ],
        scratch_shapes=[pltpu.VMEM((tm, d), BF16)],
        compiler_params=_params("arbitrary", "arbitrary"),
        name="peer_route",
    )(x2, g.reshape(1, d), w_q.astype(BF16), keys.astype(BF16))


def _sc_mesh():
    return plsc.VectorSubcoreMesh(core_axis_name="c", subcore_axis_name="s",
                                  num_cores=SC_CORES, num_subcores=SC_SUBCORES)


def _sc_worker_id():
    return lax.axis_index("s") * SC_CORES + lax.axis_index("c")


def _sc_gather_loop(tab_hbm, idx_v, buf, sems, compute):
    per_tok = idx_v.shape[1] // SC_ROWS
    n_batches = SC_GROUP * per_tok

    def gather(j, slot):
        tok = j // per_tok
        row0 = pl.multiple_of((j % per_tok) * SC_ROWS, SC_ROWS)
        return tok, row0, pltpu.make_async_copy(
            tab_hbm.at[idx_v.at[tok, pl.ds(row0, SC_ROWS)]], buf.at[slot], sems.at[slot])

    gather(0, 0)[2].start()

    def pair(jj, carry):
        j0 = jj * 2
        gather(j0 + 1, 1)[2].start()
        tok, row0, cp = gather(j0, 0)
        cp.wait()
        compute(tok, row0, buf.at[0])

        @pl.when(j0 + 2 < n_batches)
        def _():
            gather(j0 + 2, 0)[2].start()

        tok, row0, cp = gather(j0 + 1, 1)
        cp.wait()
        compute(tok, row0, buf.at[1])
        return carry

    lax.fori_loop(0, n_batches // 2, pair, 0)


def _sc_udot_body(groups_per_worker, u_hbm, idx_hbm, x_hbm, a_hbm, idx_v, x_v, a_v, buf, tr, sems):
    g0 = _sc_worker_id() * groups_per_worker
    lanes = lax.iota(jnp.int32, SC_LANES)
    chunks = x_v.shape[1] * x_v.shape[2] // SC_LANES
    per_sub = x_v.shape[2] // SC_LANES

    def compute(tok, row0, rows):
        for eb in range(SC_ROWS // SC_LANES):
            def body(c, accs):
                s, l = c // per_sub, (c % per_sub) * SC_LANES
                xj = x_v[tok, s, pl.ds(l, SC_LANES)]
                return tuple(accs[e] + xj * rows[eb * SC_LANES + e, s, pl.ds(l, SC_LANES)]
                             for e in range(SC_LANES))

            accs = lax.fori_loop(0, chunks, body, tuple(jnp.zeros((SC_LANES,), F32) for _ in range(SC_LANES)))
            for e in range(SC_LANES):
                tr[e, :] = accs[e]
            tot = jnp.zeros((SC_LANES,), F32)
            for c in range(SC_LANES):
                tot = tot + plsc.load_gather(tr, [lanes, jnp.full((SC_LANES,), c, jnp.int32)])
            a_v[tok, pl.ds(row0 + eb * SC_LANES, SC_LANES)] = tot

    def group(g, carry):
        gi = g0 + g
        pltpu.sync_copy(idx_hbm.at[gi], idx_v)
        pltpu.sync_copy(x_hbm.at[pl.ds(gi * SC_GROUP, SC_GROUP)], x_v)
        _sc_gather_loop(u_hbm, idx_v, buf, sems, compute)
        pltpu.sync_copy(a_v, a_hbm.at[gi])
        return carry

    lax.fori_loop(0, groups_per_worker, group, 0)


def _sc_vsum_body(groups_per_worker, v_hbm, idx_hbm, w_hbm, o_hbm, idx_v, w_v, o_v, buf, sems):
    g0 = _sc_worker_id() * groups_per_worker
    n_sub, per_sub = o_v.shape[1], o_v.shape[2] // SC_LANES
    n_dgroups = n_sub * per_sub // SC_LANES

    def compute(tok, row0, rows):
        tok_v = jnp.full((SC_LANES,), tok, jnp.int32)
        for dg in range(n_dgroups):
            where = [(dg * SC_LANES + cc) // per_sub for cc in range(SC_LANES)]
            lane0 = [((dg * SC_LANES + cc) % per_sub) * SC_LANES for cc in range(SC_LANES)]

            def body(r, accs):
                wk = plsc.load_gather(w_v, [tok_v, jnp.full((SC_LANES,), row0 + r, jnp.int32)])
                return tuple(accs[cc] + wk * rows[r, where[cc], pl.ds(lane0[cc], SC_LANES)]
                             for cc in range(SC_LANES))

            accs = lax.fori_loop(0, SC_ROWS, body, tuple(jnp.zeros((SC_LANES,), F32) for _ in range(SC_LANES)))
            for cc in range(SC_LANES):
                sl = (tok, where[cc], pl.ds(lane0[cc], SC_LANES))
                o_v[sl] = o_v[sl] + accs[cc]

    def group(g, carry):
        gi = g0 + g
        pltpu.sync_copy(idx_hbm.at[gi], idx_v)
        pltpu.sync_copy(w_hbm.at[gi], w_v)

        def zero(i, c):
            o_v[i // (n_sub * per_sub), (i // per_sub) % n_sub, pl.ds((i % per_sub) * SC_LANES, SC_LANES)] = (
                jnp.zeros((SC_LANES,), F32))
            return c

        lax.fori_loop(0, SC_GROUP * n_sub * per_sub, zero, 0)
        _sc_gather_loop(v_hbm, idx_v, buf, sems, compute)
        pltpu.sync_copy(o_v, o_hbm.at[pl.ds(gi * SC_GROUP, SC_GROUP)])
        return carry

    lax.fori_loop(0, groups_per_worker, group, 0)


def _gate_body(a_ref, g_ref, w_ref):
    a = a_ref[...]
    w_ref[...] = g_ref[...] * (0.5 * a * (1.0 + lax.erf(a * (2.0 ** -0.5))))


def _peer_retrieve(xn, idx, gate, u_tab, v_tab):
    t, d = xn.shape
    hk = idx.shape[1]
    n_sub = d // LANES
    n_groups = t // SC_GROUP
    groups_per_worker = n_groups // (SC_CORES * SC_SUBCORES)
    idx3 = idx.reshape(n_groups, SC_GROUP, hk)
    rows = lambda a: a.reshape(a.shape[0], n_sub, LANES)
    row_buf = pltpu.VMEM((2, SC_ROWS, n_sub, LANES), F32)
    sems = pltpu.SemaphoreType.DMA((2,))

    a3 = pl.kernel(
        functools.partial(_sc_udot_body, groups_per_worker),
        out_type=jax.ShapeDtypeStruct((n_groups, SC_GROUP, hk), F32),
        mesh=_sc_mesh(),
        scratch_types=[pltpu.VMEM((SC_GROUP, hk), jnp.int32), pltpu.VMEM((SC_GROUP, n_sub, LANES), F32),
                       pltpu.VMEM((SC_GROUP, hk), F32), row_buf, pltpu.VMEM((SC_LANES, SC_LANES), F32), sems],
        compiler_params=pltpu.CompilerParams(needs_layout_passes=False),
        name="peer_udot",
    )(rows(u_tab), idx3, rows(xn))

    g3 = gate.reshape(n_groups, SC_GROUP, hk)
    gb = min(n_groups, 256)
    blk = pl.BlockSpec((gb, SC_GROUP, hk), lambda i: (i, 0, 0))
    w3 = pl.pallas_call(
        _gate_body, grid=(n_groups // gb,), in_specs=[blk, blk], out_specs=blk,
        out_shape=jax.ShapeDtypeStruct(a3.shape, F32), compiler_params=_params("arbitrary"), name="peer_gate",
    )(a3, g3)

    o3 = pl.kernel(
        functools.partial(_sc_vsum_body, groups_per_worker),
        out_type=jax.ShapeDtypeStruct((t, n_sub, LANES), F32),
        mesh=_sc_mesh(),
        scratch_types=[pltpu.VMEM((SC_GROUP, hk), jnp.int32), pltpu.VMEM((SC_GROUP, hk), F32),
                       pltpu.VMEM((SC_GROUP, n_sub, LANES), F32), row_buf, sems],
        compiler_params=pltpu.CompilerParams(needs_layout_passes=False),
        name="peer_vsum",
    )(rows(v_tab), idx3, w3)
    return o3.reshape(t, d)


def _peer(x, g, w_q, keys, u_tab, v_tab):
    bsz, s_len, d = x.shape
    xn, idx_t, gate_t = _peer_route(x.reshape(bsz * s_len, d), g, w_q, keys)
    return _peer_retrieve(xn, idx_t.T, gate_t.T, u_tab, v_tab).reshape(bsz, s_len, d)


def _kv_body(fox_w, x_ref, po_ref, g_ref, w_ref, bf_ref, k_ref, v_ref, c_ref, ct_ref, carry):
    x = x_ref[0] + po_ref[0]
    tm = x.shape[0]
    hs = _rms(x, g_ref[...]).astype(BF16)
    z = jnp.dot(hs, w_ref[...], preferred_element_type=F32)
    k_ref[0] = z[:, :fox_w].astype(BF16)
    v_ref[0] = z[:, fox_w:2 * fox_w].astype(BF16)
    zf = z[:, 2 * fox_w:] + bf_ref[...]
    logf = jnp.minimum(zf, 0.0) - jnp.log1p(jnp.exp(-jnp.abs(zf)))

    @pl.when(pl.program_id(1) == 0)
    def _():
        carry[...] = jnp.zeros(carry.shape, F32)

    r = lax.broadcasted_iota(jnp.int32, (tm, tm), 0)
    c = lax.broadcasted_iota(jnp.int32, (tm, tm), 1)
    tri = (c <= r).astype(F32)
    cs = jnp.dot(tri, logf, preferred_element_type=F32, precision=lax.Precision.HIGHEST) + carry[...]
    carry[...] = cs[tm - 1:tm, :]
    c_ref[0] = cs
    ct_ref[0] = cs.T[0:ct_ref.shape[1], :]


def _kv_proj(x, po, g, w_kv, b_f):
    bsz, s_len, d = x.shape
    tm = min(TOKEN_TILE, s_len)
    n_heads = b_f.shape[0]
    fox_w = (w_kv.shape[1] - n_heads) // 2
    w = jnp.pad(w_kv, ((0, 0), (0, LANES - n_heads))).astype(BF16)
    bf = jnp.pad(b_f, (0, LANES - n_heads)).reshape(1, LANES)
    hp = -(-n_heads // SUBLANES) * SUBLANES
    tok = lambda w_, dt: (pl.BlockSpec((1, tm, w_), lambda b, i: (b, i, 0)), jax.ShapeDtypeStruct((bsz, s_len, w_), dt))
    (ks, ksh), (vs, vsh), (cs, csh) = tok(fox_w, BF16), tok(fox_w, BF16), tok(LANES, F32)
    return pl.pallas_call(
        functools.partial(_kv_body, fox_w),
        grid=(bsz, s_len // tm),
        in_specs=[
            pl.BlockSpec((1, tm, d), lambda b, i: (b, i, 0)),
            pl.BlockSpec((1, tm, d), lambda b, i: (b, i, 0)),
            pl.BlockSpec((1, d), lambda b, i: (0, 0)),
            pl.BlockSpec(w.shape, lambda b, i: (0, 0)),
            pl.BlockSpec((1, LANES), lambda b, i: (0, 0)),
        ],
        out_specs=[ks, vs, cs, pl.BlockSpec((1, hp, tm), lambda b, i: (b, 0, i))],
        out_shape=[ksh, vsh, csh, jax.ShapeDtypeStruct((bsz, hp, s_len), F32)],
        scratch_shapes=[pltpu.VMEM((1, LANES), F32)],
        compiler_params=_params("arbitrary", "arbitrary"),
        name="kv_proj",
    )(x, po, g.reshape(1, d), w, bf)


def _mix_b_in_body(fox_w, n_mem_heads, n_mem, x_ref, po_ref, g_ref, win_ref, mk_ref, mv_ref, q_ref, ym_ref):
    x = x_ref[0] + po_ref[0]
    h = _rms(x, g_ref[...]).astype(BF16)
    z = jnp.dot(h, win_ref[...], preferred_element_type=F32)
    q_ref[0] = (z[:, :fox_w] * (HEAD_DIM ** -0.5)).astype(BF16)
    ym_ref[0] = _mem_attend(z[:, fox_w:], mk_ref[0], mv_ref[0], n_mem_heads, n_mem).astype(BF16)


def _mix_b_in(x, po, g, w_in, mkbd, mvbd):
    bsz, s_len, d = x.shape
    tm = min(TOKEN_TILE, s_len)
    mem_w = mvbd.shape[2]
    fox_w = w_in.shape[1] - mem_w
    n_mem_heads = mem_w // HEAD_DIM
    n_mem = mkbd.shape[2] // n_mem_heads
    win = w_in.astype(BF16)
    tok = lambda w_: pl.BlockSpec((1, tm, w_), lambda b, i: (b, i, 0))
    per_b = lambda a: pl.BlockSpec((1,) + a.shape[1:], lambda b, i: (b, 0, 0))
    return pl.pallas_call(
        functools.partial(_mix_b_in_body, fox_w, n_mem_heads, n_mem),
        grid=(bsz, s_len // tm),
        in_specs=[tok(d), tok(d), pl.BlockSpec((1, d), lambda b, i: (0, 0)),
                  pl.BlockSpec(win.shape, lambda b, i: (0, 0)), per_b(mkbd), per_b(mvbd)],
        out_specs=[tok(fox_w), tok(mem_w)],
        out_shape=[jax.ShapeDtypeStruct((bsz, s_len, fox_w), BF16),
                   jax.ShapeDtypeStruct((bsz, s_len, mem_w), BF16)],
        compiler_params=_params("arbitrary", "arbitrary"),
        name="mix_b_in",
    )(x, po, g.reshape(1, d), win, mkbd, mvbd)


def _fox_body(n_heads, q_ref, k_ref, v_ref, c_ref, ct_ref, o_ref, m_scr, l_scr, acc):
    qi, ki = pl.program_id(1), pl.program_id(2)
    tq, tk = q_ref.shape[1], k_ref.shape[1]

    @pl.when(ki == 0)
    def _():
        m_scr[...] = jnp.full(m_scr.shape, -jnp.inf, F32)
        l_scr[...] = jnp.zeros(l_scr.shape, F32)
        acc[...] = jnp.zeros(acc.shape, F32)

    def step(masked):
        if masked:
            keep = (lax.broadcasted_iota(jnp.int32, (tq, tk), 0)
                    >= lax.broadcasted_iota(jnp.int32, (tq, tk), 1))
        for h in range(n_heads):
            sl = slice(h * HEAD_DIM, (h + 1) * HEAD_DIM)
            s = lax.dot_general(q_ref[0, :, sl], k_ref[0, :, sl], (((1,), (1,)), ((), ())),
                                preferred_element_type=F32)
            s = s + (c_ref[0, :, h:h + 1] - ct_ref[0, h:h + 1, :])
            if masked:
                s = jnp.where(keep, s, -jnp.inf)
            m_old = m_scr[:, h:h + 1]
            m_new = jnp.maximum(m_old, jnp.max(s, axis=-1, keepdims=True))
            alpha = jnp.exp(m_old - m_new)
            p = jnp.exp(s - m_new)
            l_scr[:, h:h + 1] = alpha * l_scr[:, h:h + 1] + jnp.sum(p, axis=-1, keepdims=True)
            acc[:, sl] = alpha * acc[:, sl] + jnp.dot(p.astype(BF16), v_ref[0, :, sl],
                                                      preferred_element_type=F32)
            m_scr[:, h:h + 1] = m_new

    @pl.when(ki < qi)
    def _():
        step(False)

    @pl.when(ki == qi)
    def _():
        step(True)
        for h in range(n_heads):
            sl = slice(h * HEAD_DIM, (h + 1) * HEAD_DIM)
            o_ref[0, :, sl] = (acc[:, sl] / l_scr[:, h:h + 1]).astype(BF16)


def _fox(q, k, v, c, ct):
    bsz, s_len, w = q.shape
    n_heads = w // HEAD_DIM
    t = min(ATTN_TILE, s_len)
    nb = s_len // t
    return pl.pallas_call(
        functools.partial(_fox_body, n_heads),
        grid=(bsz, nb, nb),
        in_specs=[
            pl.BlockSpec((1, t, w), lambda b, i, j: (b, i, 0)),
            pl.BlockSpec((1, t, w), lambda b, i, j: (b, jnp.minimum(i, j), 0)),
            pl.BlockSpec((1, t, w), lambda b, i, j: (b, jnp.minimum(i, j), 0)),
            pl.BlockSpec((1, t, c.shape[2]), lambda b, i, j: (b, i, 0)),
            pl.BlockSpec((1, ct.shape[1], t), lambda b, i, j: (b, 0, jnp.minimum(i, j))),
        ],
        out_specs=pl.BlockSpec((1, t, w), lambda b, i, j: (b, i, 0)),
        out_shape=jax.ShapeDtypeStruct((bsz, s_len, w), BF16),
        scratch_shapes=[pltpu.VMEM((t, LANES), F32), pltpu.VMEM((t, LANES), F32), pltpu.VMEM((t, w), F32)],
        compiler_params=_params("arbitrary", "arbitrary", "arbitrary"),
        name="fox",
    )(q, k, v, c, ct)


def _mix_b_out_body(fox_w, x_ref, po_ref, y_ref, ym_ref, wout_ref, o_ref):
    o_ref[0] = (x_ref[0] + po_ref[0]
                + jnp.dot(y_ref[0], wout_ref[0:fox_w, :], preferred_element_type=F32)
                + jnp.dot(ym_ref[0], wout_ref[fox_w:, :], preferred_element_type=F32))


def _mix_b_out(x, po, y, ym, w_out):
    bsz, s_len, d = x.shape
    tm = min(TOKEN_TILE, s_len)
    fox_w = y.shape[2]
    wout = w_out.astype(BF16)
    tok = lambda w_: pl.BlockSpec((1, tm, w_), lambda b, i: (b, i, 0))
    return pl.pallas_call(
        functools.partial(_mix_b_out_body, fox_w),
        grid=(bsz, s_len // tm),
        in_specs=[tok(d), tok(d), tok(fox_w), tok(ym.shape[2]), pl.BlockSpec(wout.shape, lambda b, i: (0, 0))],
        out_specs=tok(d),
        out_shape=jax.ShapeDtypeStruct(x.shape, F32),
        compiler_params=_params("arbitrary", "arbitrary"),
        name="mix_b_out",
    )(x, po, y, ym, wout)


def _final_body(x_ref, po_ref, g_ref, o_ref):
    o_ref[0] = _rms(x_ref[0] + po_ref[0], g_ref[...])


def _final_norm(x, po, g):
    bsz, s_len, d = x.shape
    tm = min(TOKEN_TILE, s_len)
    tok = pl.BlockSpec((1, tm, d), lambda b, i: (b, i, 0))
    return pl.pallas_call(
        _final_body,
        grid=(bsz, s_len // tm),
        in_specs=[tok, tok, pl.BlockSpec((1, d), lambda b, i: (0, 0))],
        out_specs=tok,
        out_shape=jax.ShapeDtypeStruct(x.shape, F32),
        compiler_params=_params("arbitrary", "arbitrary"),
        name="final_norm",
    )(x, po, g.reshape(1, d))


def kernel(x, mem, g_mix, w_in_a, conv_w, w_in_b, w_out, g_mem, w_mem_kv, g_kv, w_kv, b_f, g_ffn,
           peer_wq, peer_keys, peer_u, peer_v, g_final):
    depth = g_mix.shape[0]
    n_a = w_in_a.shape[0]
    n_mem_heads = w_mem_kv.shape[2] // (2 * HEAD_DIM)
    mkbd, mvbd = _block_diag_mem(_memkv(mem, g_mem, w_mem_kv), n_mem_heads)

    po = None
    k_sh = v_sh = c_sh = ct_sh = None
    for layer in range(depth):
        if layer == n_a:
            k_sh, v_sh, c_sh, ct_sh = _kv_proj(x, po, g_kv, w_kv, b_f)
        if layer < n_a:
            x = _mix_a(x, po, g_mix[layer], w_in_a[layer], conv_w[layer], mkbd[layer], mvbd[layer], w_out[layer])
        else:
            q, ym = _mix_b_in(x, po, g_mix[layer], w_in_b[layer - n_a], mkbd[layer], mvbd[layer])
            y = _fox(q, k_sh, v_sh, c_sh, ct_sh)
            x = _mix_b_out(x, po, y, ym, w_out[layer])
        po = _peer(x, g_ffn[layer], peer_wq[layer], peer_keys[layer], peer_u[layer], peer_v[layer])
    return _final_norm(x, po, g_final)
```

```python
import functools

import jax
import jax.numpy as jnp
from jax import lax
from jax.experimental import pallas as pl
from jax.experimental.pallas import tpu as pltpu
from jax.experimental.pallas import tpu_sc as plsc

EPS = 1e-6
HEAD_DIM = 64
CONV_K = 3
PEER_HEADS = 8
PEER_NKEYS = 128
PEER_HALF = 128
PEER_TOPK = 16
PEER_CHUNK = 128
LANES = 128
SUBLANES = 8
TOKEN_TILE = 512
ROUTE_TILE = 256
ATTN_TILE = 512
VMEM_LIMIT = 56 * 1024 * 1024
SC_CORES = 2
SC_SUBCORES = 16
SC_LANES = 16
SC_ROWS = 64
SC_GROUP = 8
BATCH_STREAMS = 4

BF16 = jnp.bfloat16
F32 = jnp.float32


def _params(*sem):
    return pltpu.CompilerParams(dimension_semantics=sem, vmem_limit_bytes=VMEM_LIMIT)


def _rms(x, g):
    return x * lax.rsqrt(jnp.mean(x * x, axis=-1, keepdims=True) + EPS) * g


def _mem_attend(qm, mkbd, mvbd, n_heads, n_mem):
    lg = jnp.dot(qm.astype(BF16), mkbd, preferred_element_type=F32) * (HEAD_DIM ** -0.5)
    parts = []
    for h in range(n_heads):
        l = lg[:, h * n_mem:(h + 1) * n_mem]
        e = jnp.exp(l - jnp.max(l, axis=-1, keepdims=True))
        parts.append((e / jnp.sum(e, axis=-1, keepdims=True)).astype(BF16))
    return jnp.dot(jnp.concatenate(parts, axis=1), mvbd, preferred_element_type=F32)


def _memkv_body(mem_ref, g_ref, w_ref, o_ref):
    mn = _rms(mem_ref[0], g_ref[0])
    o_ref[0, 0] = jnp.dot(mn.astype(BF16), w_ref[0], preferred_element_type=F32).astype(BF16)


def _memkv(mem, g_mem, w_mem_kv):
    bsz, n_mem, d = mem.shape
    depth, _, wd = w_mem_kv.shape
    return pl.pallas_call(
        _memkv_body,
        grid=(depth, bsz),
        in_specs=[
            pl.BlockSpec((1, n_mem, d), lambda l, b: (b, 0, 0)),
            pl.BlockSpec((1, 1, d), lambda l, b: (l, 0, 0)),
            pl.BlockSpec((1, d, wd), lambda l, b: (l, 0, 0)),
        ],
        out_specs=pl.BlockSpec((1, 1, n_mem, wd), lambda l, b: (l, b, 0, 0)),
        out_shape=jax.ShapeDtypeStruct((depth, bsz, n_mem, wd), BF16),
        compiler_params=_params("arbitrary", "arbitrary"),
        name="memkv",
    )(mem, g_mem.reshape(depth, 1, d), w_mem_kv.astype(BF16))


def _block_diag_mem(mkv, n_heads):
    depth, bsz, n_mem, _ = mkv.shape
    r = mkv.reshape(depth, bsz, n_mem, 2, n_heads, HEAD_DIM)
    eye = jnp.eye(n_heads, dtype=mkv.dtype)
    mk = jnp.einsum("lbmhd,hg->lbhdgm", r[:, :, :, 0], eye)
    mv = jnp.einsum("lbmhd,hg->lbhmgd", r[:, :, :, 1], eye)
    hw = n_heads * HEAD_DIM
    return mk.reshape(depth, bsz, hw, n_heads * n_mem), mv.reshape(depth, bsz, n_heads * n_mem, hw)


def _mix_a_body(has_po, conv_ch, n_mem_heads, n_mem, *refs):
    if has_po:
        x_ref, po_ref, g_ref, win_ref, cw_ref, mk_ref, mv_ref, wout_ref, o_ref, cbuf = refs
        x = x_ref[0] + po_ref[0]
    else:
        x_ref, g_ref, win_ref, cw_ref, mk_ref, mv_ref, wout_ref, o_ref, cbuf = refs
        x = x_ref[0]
    tm = x.shape[0]
    h = _rms(x, g_ref[...]).astype(BF16)
    z = jnp.dot(h, win_ref[...], preferred_element_type=F32)
    u = z[:, :conv_ch]
    c_gate = z[:, conv_ch:2 * conv_ch]
    b_gate = z[:, 2 * conv_ch:3 * conv_ch]
    qm = z[:, 3 * conv_ch:]
    cu = c_gate * u

    @pl.when(pl.program_id(1) == 0)
    def _():
        cbuf[0:SUBLANES, :] = jnp.zeros((SUBLANES, conv_ch), F32)

    cbuf[SUBLANES:SUBLANES + tm, :] = cu
    cu1 = cbuf[SUBLANES - 1:SUBLANES - 1 + tm, :]
    cu2 = cbuf[SUBLANES - 2:SUBLANES - 2 + tm, :]
    cw = cw_ref[...]
    y = b_gate * (cw[0:1] * cu2 + cw[1:2] * cu1 + cw[2:3] * cu)
    cbuf[0:SUBLANES, :] = cbuf[tm:tm + SUBLANES, :]

    ymem = _mem_attend(qm, mk_ref[0], mv_ref[0], n_mem_heads, n_mem)
    o_ref[0] = (x
                + jnp.dot(y.astype(BF16), wout_ref[0:conv_ch, :], preferred_element_type=F32)
                + jnp.dot(ymem.astype(BF16), wout_ref[conv_ch:, :], preferred_element_type=F32))


def _mix_a(x, po, g, w_in, conv_w, mkbd, mvbd, w_out):
    bsz, s_len, d = x.shape
    tm = min(TOKEN_TILE, s_len)
    conv_ch = conv_w.shape[1]
    mem_w = w_in.shape[1] - 3 * conv_ch
    n_mem_heads = mem_w // HEAD_DIM
    n_mem = mkbd.shape[2] // n_mem_heads
    tok = pl.BlockSpec((1, tm, d), lambda b, i: (b, i, 0))
    full = lambda a: pl.BlockSpec(a.shape, lambda b, i: (0,) * a.ndim)
    per_b = lambda a: pl.BlockSpec((1,) + a.shape[1:], lambda b, i: (b, 0, 0))
    g2 = g.reshape(1, d)
    win = w_in.astype(BF16)
    wout = w_out.astype(BF16)
    args = [x] + ([po] if po is not None else []) + [g2, win, conv_w, mkbd, mvbd, wout]
    specs = [tok] + ([tok] if po is not None else []) + [full(g2), full(win), full(conv_w), per_b(mkbd), per_b(mvbd), full(wout)]
    return pl.pallas_call(
        functools.partial(_mix_a_body, po is not None, conv_ch, n_mem_heads, n_mem),
        grid=(bsz, s_len // tm),
        in_specs=specs,
        out_specs=tok,
        out_shape=jax.ShapeDtypeStruct(x.shape, F32),
        scratch_shapes=[pltpu.VMEM((tm + SUBLANES, conv_ch), F32)],
        compiler_params=_params("arbitrary", "arbitrary"),
        name="mix_a",
    )(*args)


def _topk_rows(s, k, payload=None):
    n = s.shape[0]
    rows = lax.broadcasted_iota(jnp.int32, s.shape, 0)
    vals, outs = [], []
    for _ in range(k):
        m = jnp.max(s, axis=0, keepdims=True)
        pos = jnp.min(jnp.where(s == m, rows, n), axis=0, keepdims=True)
        sel = rows == pos
        vals.append(m)
        if payload is None:
            outs.append(pos)
        else:
            outs.append(jnp.sum(jnp.where(sel, payload, 0), axis=0, keepdims=True))
        s = jnp.where(sel, -jnp.inf, s)
    return jnp.concatenate(vals, axis=0), jnp.concatenate(outs, axis=0)


def _route_body(x_ref, g_ref, wq_ref, keys_ref, xn_ref, idx_ref, gate_ref, xnb):
    @pl.when(pl.program_id(1) == 0)
    def _():
        xn = _rms(x_ref[...], g_ref[...])
        xn_ref[...] = xn
        xnb[...] = xn.astype(BF16)

    q = jnp.dot(xnb[...], wq_ref[...], preferred_element_type=F32)
    tops = []
    for p in range(2):
        qp = q[:, p * PEER_HALF:(p + 1) * PEER_HALF].astype(BF16)
        st = lax.dot_general(keys_ref[0, p], qp, (((1,), (1,)), ((), ())),
                             preferred_element_type=F32)
        tops.append(_topk_rows(st, PEER_TOPK))
    (v1, i1), (v2, i2) = tops
    cand = jnp.concatenate([v1[i:i + 1] + v2 for i in range(PEER_TOPK)], axis=0)
    cidx = jnp.concatenate([i1[i:i + 1] * PEER_NKEYS + i2 for i in range(PEER_TOPK)], axis=0)
    top_s, idx = _topk_rows(cand, PEER_TOPK, payload=cidx)
    e = jnp.exp(top_s - top_s[0:1])
    idx_ref[...] = idx
    gate_ref[...] = e / jnp.sum(e, axis=0, keepdims=True)


def _peer_route(x2, g, w_q, keys):
    t, d = x2.shape
    tm = min(ROUTE_TILE, t)
    dk = 2 * PEER_HALF
    return pl.pallas_call(
        _route_body,
        grid=(t // tm, PEER_HEADS),
        in_specs=[
            pl.BlockSpec((tm, d), lambda i, h: (i, 0)),
            pl.BlockSpec((1, d), lambda i, h: (0, 0)),
            pl.BlockSpec((d, dk), lambda i, h: (0, h)),
            pl.BlockSpec((1, 2, PEER_NKEYS, PEER_HALF), lambda i, h: (h, 0, 0, 0)),
        ],
        out_specs=[
            pl.BlockSpec((tm, d), lambda i, h: (i, 0)),
            pl.BlockSpec((PEER_TOPK, tm), lambda i, h: (h, i)),
            pl.BlockSpec((PEER_TOPK, tm), lambda i, h: (h, i)),
        ],
        out_shape=[
            jax.ShapeDtypeStruct((t, d), F32),
            jax.ShapeDtypeStruct((PEER_HEADS * PEER_TOPK, t), jnp.int32),
            jax.ShapeDtypeStruct((PEER_HEADS * PEER_TOPK, t), F32),
        ],
        scratch_shapes=[pltpu.VMEM((tm, d), BF16)],
        compiler_params=_params("arbitrary", "arbitrary"),
        name="peer_route",
    )(x2, g.reshape(1, d), w_q.astype(BF16), keys.astype(BF16))


def _sc_mesh():
    return plsc.VectorSubcoreMesh(core_axis_name="c", subcore_axis_name="s",
                                  num_cores=SC_CORES, num_subcores=SC_SUBCORES)


def _sc_worker_id():
    return lax.axis_index("s") * SC_CORES + lax.axis_index("c")


def _sc_gather_loop(tab_hbm, idx_v, buf, sems, compute):
    per_tok = idx_v.shape[1] // SC_ROWS
    n_batches = SC_GROUP * per_tok

    def gather(j, slot):
        tok = j // per_tok
        row0 = pl.multiple_of((j % per_tok) * SC_ROWS, SC_ROWS)
        return tok, row0, pltpu.make_async_copy(
            tab_hbm.at[idx_v.at[tok, pl.ds(row0, SC_ROWS)]], buf.at[slot], sems.at[slot])

    nbuf = buf.shape[0]
    for b in range(nbuf - 1):
        gather(b, b)[2].start()

    def ring(jj, carry):
        for b in range(nbuf):
            j = jj * nbuf + b
            ahead = j + nbuf - 1

            @pl.when(ahead < n_batches)
            def _():
                gather(ahead, (b + nbuf - 1) % nbuf)[2].start()

            tok, row0, cp = gather(j, b)
            cp.wait()
            compute(tok, row0, buf.at[b])
        return carry

    lax.fori_loop(0, n_batches // nbuf, ring, 0)


def _pack_rows(tab):
    n, d = tab.shape
    bits = lax.bitcast_convert_type(tab.astype(BF16).reshape(n, d // (2 * SC_LANES), 2, SC_LANES), jnp.uint16)
    bits = bits.astype(jnp.uint32)
    words = (bits[:, :, 1, :] << 16) | bits[:, :, 0, :]
    return lax.bitcast_convert_type(words, jnp.int32).reshape(n, d // 2)


def _unpack_words(w):
    lo = lax.bitcast_convert_type(w << 16, F32)
    hi = lax.bitcast_convert_type(w & jnp.int32(-65536), F32)
    return lo, hi


def _sc_udot_body(groups_per_worker, u_hbm, idx_hbm, x_hbm, a_hbm, idx_v, x_v, a_v, buf, tr, sems):
    g0 = _sc_worker_id() * groups_per_worker
    lanes = lax.iota(jnp.int32, SC_LANES)
    chunks = x_v.shape[1] // (2 * SC_LANES)

    def compute(tok, row0, rows):
        for eb in range(SC_ROWS // SC_LANES):
            def body(c, accs):
                x_lo = x_v[tok, pl.ds(c * 2 * SC_LANES, SC_LANES)]
                x_hi = x_v[tok, pl.ds(c * 2 * SC_LANES + SC_LANES, SC_LANES)]
                out = []
                for e in range(SC_LANES):
                    lo, hi = _unpack_words(rows[eb * SC_LANES + e, pl.ds(c * SC_LANES, SC_LANES)])
                    out.append(accs[e] + lo * x_lo + hi * x_hi)
                return tuple(out)

            accs = lax.fori_loop(0, chunks, body, tuple(jnp.zeros((SC_LANES,), F32) for _ in range(SC_LANES)))
            for e in range(SC_LANES):
                tr[e, :] = accs[e]
            tot = jnp.zeros((SC_LANES,), F32)
            for c in range(SC_LANES):
                tot = tot + plsc.load_gather(tr, [lanes, jnp.full((SC_LANES,), c, jnp.int32)])
            a_v[tok, pl.ds(row0 + eb * SC_LANES, SC_LANES)] = tot

    def group(g, carry):
        gi = g0 + g
        pltpu.sync_copy(idx_hbm.at[gi], idx_v)
        pltpu.sync_copy(x_hbm.at[pl.ds(gi * SC_GROUP, SC_GROUP)], x_v)
        _sc_gather_loop(u_hbm, idx_v, buf, sems, compute)
        pltpu.sync_copy(a_v, a_hbm.at[gi])
        return carry

    lax.fori_loop(0, groups_per_worker, group, 0)


def _sc_vsum_body(groups_per_worker, v_hbm, idx_hbm, w_hbm, o_hbm, idx_v, w_v, o_v, buf, sems):
    g0 = _sc_worker_id() * groups_per_worker
    d = o_v.shape[1]
    words_per_pass = SC_LANES // 2
    n_passes = d // (2 * SC_LANES * words_per_pass)

    def compute(tok, row0, rows):
        tok_v = jnp.full((SC_LANES,), tok, jnp.int32)
        for dg in range(n_passes):
            def body(r, accs):
                wk = plsc.load_gather(w_v, [tok_v, jnp.full((SC_LANES,), row0 + r, jnp.int32)])
                out = []
                for cc in range(words_per_pass):
                    lo, hi = _unpack_words(rows[r, pl.ds((dg * words_per_pass + cc) * SC_LANES, SC_LANES)])
                    out += [accs[2 * cc] + wk * lo, accs[2 * cc + 1] + wk * hi]
                return tuple(out)

            accs = lax.fori_loop(0, SC_ROWS, body, tuple(jnp.zeros((SC_LANES,), F32) for _ in range(SC_LANES)))
            for cc in range(SC_LANES):
                sl = (tok, pl.ds((dg * SC_LANES + cc) * SC_LANES, SC_LANES))
                o_v[sl] = o_v[sl] + accs[cc]

    def group(g, carry):
        gi = g0 + g
        pltpu.sync_copy(idx_hbm.at[gi], idx_v)
        pltpu.sync_copy(w_hbm.at[gi], w_v)

        def zero(i, c):
            o_v[i // (d // SC_LANES), pl.ds((i % (d // SC_LANES)) * SC_LANES, SC_LANES)] = jnp.zeros((SC_LANES,), F32)
            return c

        lax.fori_loop(0, SC_GROUP * (d // SC_LANES), zero, 0)
        _sc_gather_loop(v_hbm, idx_v, buf, sems, compute)
        pltpu.sync_copy(o_v, o_hbm.at[pl.ds(gi * SC_GROUP, SC_GROUP)])
        return carry

    lax.fori_loop(0, groups_per_worker, group, 0)


def _gate_body(a_ref, g_ref, w_ref):
    a = a_ref[...]
    w_ref[...] = g_ref[...] * (0.5 * a * (1.0 + lax.erf(a * (2.0 ** -0.5))))


def _peer_retrieve(xn, idx, gate, u_tab, v_tab):
    t, d = xn.shape
    hk = idx.shape[1]
    n_groups = t // SC_GROUP
    groups_per_worker = n_groups // (SC_CORES * SC_SUBCORES)
    idx3 = idx.reshape(n_groups, SC_GROUP, hk)
    row_buf = pltpu.VMEM((2, SC_ROWS, d // 2), jnp.int32)
    sems = pltpu.SemaphoreType.DMA((2,))
    cost = pl.CostEstimate(flops=2 * t * hk * d, transcendentals=0,
                           bytes_accessed=2 * t * hk * d + 4 * (t * d + 2 * t * hk))

    a3 = pl.kernel(
        functools.partial(_sc_udot_body, groups_per_worker),
        out_type=jax.ShapeDtypeStruct((n_groups, SC_GROUP, hk), F32),
        mesh=_sc_mesh(),
        scratch_types=[pltpu.VMEM((SC_GROUP, hk), jnp.int32), pltpu.VMEM((SC_GROUP, d), F32),
                       pltpu.VMEM((SC_GROUP, hk), F32), row_buf, pltpu.VMEM((SC_LANES, SC_LANES), F32), sems],
        compiler_params=pltpu.CompilerParams(needs_layout_passes=False),
        cost_estimate=cost,
        name="peer_udot",
    )(u_tab, idx3, xn)

    g3 = gate.reshape(n_groups, SC_GROUP, hk)
    gb = min(n_groups, 256)
    blk = pl.BlockSpec((gb, SC_GROUP, hk), lambda i: (i, 0, 0))
    w3 = pl.pallas_call(
        _gate_body, grid=(n_groups // gb,), in_specs=[blk, blk], out_specs=blk,
        out_shape=jax.ShapeDtypeStruct(a3.shape, F32), compiler_params=_params("arbitrary"), name="peer_gate",
    )(a3, g3)

    o3 = pl.kernel(
        functools.partial(_sc_vsum_body, groups_per_worker),
        out_type=jax.ShapeDtypeStruct((t, d), F32),
        mesh=_sc_mesh(),
        scratch_types=[pltpu.VMEM((SC_GROUP, hk), jnp.int32), pltpu.VMEM((SC_GROUP, hk), F32),
                       pltpu.VMEM((SC_GROUP, d), F32), row_buf, sems],
        compiler_params=pltpu.CompilerParams(needs_layout_passes=False),
        cost_estimate=cost,
        name="peer_vsum",
    )(v_tab, idx3, w3)
    return o3


def _peer(x, g, w_q, keys, u_tab, v_tab):
    bsz, s_len, d = x.shape
    xn, idx_t, gate_t = _peer_route(x.reshape(bsz * s_len, d), g, w_q, keys)
    return _peer_retrieve(xn, idx_t.T, gate_t.T, u_tab, v_tab).reshape(bsz, s_len, d)


def _kv_body(fox_w, x_ref, po_ref, g_ref, w_ref, bf_ref, k_ref, v_ref, c_ref, ct_ref, carry):
    x = x_ref[0] + po_ref[0]
    tm = x.shape[0]
    hs = _rms(x, g_ref[...]).astype(BF16)
    z = jnp.dot(hs, w_ref[...], preferred_element_type=F32)
    k_ref[0] = z[:, :fox_w].astype(BF16)
    v_ref[0] = z[:, fox_w:2 * fox_w].astype(BF16)
    zf = z[:, 2 * fox_w:] + bf_ref[...]
    logf = jnp.minimum(zf, 0.0) - jnp.log1p(jnp.exp(-jnp.abs(zf)))

    @pl.when(pl.program_id(1) == 0)
    def _():
        carry[...] = jnp.zeros(carry.shape, F32)

    r = lax.broadcasted_iota(jnp.int32, (tm, tm), 0)
    c = lax.broadcasted_iota(jnp.int32, (tm, tm), 1)
    tri = (c <= r).astype(F32)
    cs = jnp.dot(tri, logf, preferred_element_type=F32, precision=lax.Precision.HIGHEST) + carry[...]
    carry[...] = cs[tm - 1:tm, :]
    c_ref[0] = cs
    ct_ref[0] = cs.T[0:ct_ref.shape[1], :]


def _kv_proj(x, po, g, w_kv, b_f):
    bsz, s_len, d = x.shape
    tm = min(TOKEN_TILE, s_len)
    n_heads = b_f.shape[0]
    fox_w = (w_kv.shape[1] - n_heads) // 2
    w = jnp.pad(w_kv, ((0, 0), (0, LANES - n_heads))).astype(BF16)
    bf = jnp.pad(b_f, (0, LANES - n_heads)).reshape(1, LANES)
    hp = -(-n_heads // SUBLANES) * SUBLANES
    tok = lambda w_, dt: (pl.BlockSpec((1, tm, w_), lambda b, i: (b, i, 0)), jax.ShapeDtypeStruct((bsz, s_len, w_), dt))
    (ks, ksh), (vs, vsh), (cs, csh) = tok(fox_w, BF16), tok(fox_w, BF16), tok(LANES, F32)
    return pl.pallas_call(
        functools.partial(_kv_body, fox_w),
        grid=(bsz, s_len // tm),
        in_specs=[
            pl.BlockSpec((1, tm, d), lambda b, i: (b, i, 0)),
            pl.BlockSpec((1, tm, d), lambda b, i: (b, i, 0)),
            pl.BlockSpec((1, d), lambda b, i: (0, 0)),
            pl.BlockSpec(w.shape, lambda b, i: (0, 0)),
            pl.BlockSpec((1, LANES), lambda b, i: (0, 0)),
        ],
        out_specs=[ks, vs, cs, pl.BlockSpec((1, hp, tm), lambda b, i: (b, 0, i))],
        out_shape=[ksh, vsh, csh, jax.ShapeDtypeStruct((bsz, hp, s_len), F32)],
        scratch_shapes=[pltpu.VMEM((1, LANES), F32)],
        compiler_params=_params("arbitrary", "arbitrary"),
        name="kv_proj",
    )(x, po, g.reshape(1, d), w, bf)


def _mix_b_in_body(fox_w, n_mem_heads, n_mem, x_ref, po_ref, g_ref, win_ref, mk_ref, mv_ref, q_ref, ym_ref):
    x = x_ref[0] + po_ref[0]
    h = _rms(x, g_ref[...]).astype(BF16)
    z = jnp.dot(h, win_ref[...], preferred_element_type=F32)
    q_ref[0] = (z[:, :fox_w] * (HEAD_DIM ** -0.5)).astype(BF16)
    ym_ref[0] = _mem_attend(z[:, fox_w:], mk_ref[0], mv_ref[0], n_mem_heads, n_mem).astype(BF16)


def _mix_b_in(x, po, g, w_in, mkbd, mvbd):
    bsz, s_len, d = x.shape
    tm = min(TOKEN_TILE, s_len)
    mem_w = mvbd.shape[2]
    fox_w = w_in.shape[1] - mem_w
    n_mem_heads = mem_w // HEAD_DIM
    n_mem = mkbd.shape[2] // n_mem_heads
    win = w_in.astype(BF16)
    tok = lambda w_: pl.BlockSpec((1, tm, w_), lambda b, i: (b, i, 0))
    per_b = lambda a: pl.BlockSpec((1,) + a.shape[1:], lambda b, i: (b, 0, 0))
    return pl.pallas_call(
        functools.partial(_mix_b_in_body, fox_w, n_mem_heads, n_mem),
        grid=(bsz, s_len // tm),
        in_specs=[tok(d), tok(d), pl.BlockSpec((1, d), lambda b, i: (0, 0)),
                  pl.BlockSpec(win.shape, lambda b, i: (0, 0)), per_b(mkbd), per_b(mvbd)],
        out_specs=[tok(fox_w), tok(mem_w)],
        out_shape=[jax.ShapeDtypeStruct((bsz, s_len, fox_w), BF16),
                   jax.ShapeDtypeStruct((bsz, s_len, mem_w), BF16)],
        compiler_params=_params("arbitrary", "arbitrary"),
        name="mix_b_in",
    )(x, po, g.reshape(1, d), win, mkbd, mvbd)


def _fox_body(n_heads, q_ref, k_ref, v_ref, c_ref, ct_ref, o_ref, m_scr, l_scr, acc):
    qi, ki = pl.program_id(1), pl.program_id(2)
    tq, tk = q_ref.shape[1], k_ref.shape[1]

    @pl.when(ki == 0)
    def _():
        m_scr[...] = jnp.full(m_scr.shape, -jnp.inf, F32)
        l_scr[...] = jnp.zeros(l_scr.shape, F32)
        acc[...] = jnp.zeros(acc.shape, F32)

    def step(masked):
        if masked:
            keep = (lax.broadcasted_iota(jnp.int32, (tq, tk), 0)
                    >= lax.broadcasted_iota(jnp.int32, (tq, tk), 1))
        for h in range(n_heads):
            sl = slice(h * HEAD_DIM, (h + 1) * HEAD_DIM)
            s = lax.dot_general(q_ref[0, :, sl], k_ref[0, :, sl], (((1,), (1,)), ((), ())),
                                preferred_element_type=F32)
            s = s + (c_ref[0, :, h:h + 1] - ct_ref[0, h:h + 1, :])
            if masked:
                s = jnp.where(keep, s, -jnp.inf)
            m_old = m_scr[:, h:h + 1]
            m_new = jnp.maximum(m_old, jnp.max(s, axis=-1, keepdims=True))
            alpha = jnp.exp(m_old - m_new)
            p = jnp.exp(s - m_new)
            l_scr[:, h:h + 1] = alpha * l_scr[:, h:h + 1] + jnp.sum(p, axis=-1, keepdims=True)
            acc[:, sl] = alpha * acc[:, sl] + jnp.dot(p.astype(BF16), v_ref[0, :, sl],
                                                      preferred_element_type=F32)
            m_scr[:, h:h + 1] = m_new

    @pl.when(ki < qi)
    def _():
        step(False)

    @pl.when(ki == qi)
    def _():
        step(True)
        for h in range(n_heads):
            sl = slice(h * HEAD_DIM, (h + 1) * HEAD_DIM)
            o_ref[0, :, sl] = (acc[:, sl] / l_scr[:, h:h + 1]).astype(BF16)


def _fox(q, k, v, c, ct):
    bsz, s_len, w = q.shape
    n_heads = w // HEAD_DIM
    t = min(ATTN_TILE, s_len)
    nb = s_len // t
    return pl.pallas_call(
        functools.partial(_fox_body, n_heads),
        grid=(bsz, nb, nb),
        in_specs=[
            pl.BlockSpec((1, t, w), lambda b, i, j: (b, i, 0)),
            pl.BlockSpec((1, t, w), lambda b, i, j: (b, jnp.minimum(i, j), 0)),
            pl.BlockSpec((1, t, w), lambda b, i, j: (b, jnp.minimum(i, j), 0)),
            pl.BlockSpec((1, t, c.shape[2]), lambda b, i, j: (b, i, 0)),
            pl.BlockSpec((1, ct.shape[1], t), lambda b, i, j: (b, 0, jnp.minimum(i, j))),
        ],
        out_specs=pl.BlockSpec((1, t, w), lambda b, i, j: (b, i, 0)),
        out_shape=jax.ShapeDtypeStruct((bsz, s_len, w), BF16),
        scratch_shapes=[pltpu.VMEM((t, LANES), F32), pltpu.VMEM((t, LANES), F32), pltpu.VMEM((t, w), F32)],
        compiler_params=_params("arbitrary", "arbitrary", "arbitrary"),
        name="fox",
    )(q, k, v, c, ct)


def _mix_b_out_body(fox_w, x_ref, po_ref, y_ref, ym_ref, wout_ref, o_ref):
    o_ref[0] = (x_ref[0] + po_ref[0]
                + jnp.dot(y_ref[0], wout_ref[0:fox_w, :], preferred_element_type=F32)
                + jnp.dot(ym_ref[0], wout_ref[fox_w:, :], preferred_element_type=F32))


def _mix_b_out(x, po, y, ym, w_out):
    bsz, s_len, d = x.shape
    tm = min(TOKEN_TILE, s_len)
    fox_w = y.shape[2]
    wout = w_out.astype(BF16)
    tok = lambda w_: pl.BlockSpec((1, tm, w_), lambda b, i: (b, i, 0))
    return pl.pallas_call(
        functools.partial(_mix_b_out_body, fox_w),
        grid=(bsz, s_len // tm),
        in_specs=[tok(d), tok(d), tok(fox_w), tok(ym.shape[2]), pl.BlockSpec(wout.shape, lambda b, i: (0, 0))],
        out_specs=tok(d),
        out_shape=jax.ShapeDtypeStruct(x.shape, F32),
        compiler_params=_params("arbitrary", "arbitrary"),
        name="mix_b_out",
    )(x, po, y, ym, wout)


def _final_body(x_ref, po_ref, g_ref, o_ref):
    o_ref[0] = _rms(x_ref[0] + po_ref[0], g_ref[...])


def _final_norm(x, po, g):
    bsz, s_len, d = x.shape
    tm = min(TOKEN_TILE, s_len)
    tok = pl.BlockSpec((1, tm, d), lambda b, i: (b, i, 0))
    return pl.pallas_call(
        _final_body,
        grid=(bsz, s_len // tm),
        in_specs=[tok, tok, pl.BlockSpec((1, d), lambda b, i: (0, 0))],
        out_specs=tok,
        out_shape=jax.ShapeDtypeStruct(x.shape, F32),
        compiler_params=_params("arbitrary", "arbitrary"),
        name="final_norm",
    )(x, po, g.reshape(1, d))


def kernel(x, mem, g_mix, w_in_a, conv_w, w_in_b, w_out, g_mem, w_mem_kv, g_kv, w_kv, b_f, g_ffn,
           peer_wq, peer_keys, peer_u, peer_v, g_final):
    depth = g_mix.shape[0]
    n_a = w_in_a.shape[0]
    n_mem_heads = w_mem_kv.shape[2] // (2 * HEAD_DIM)
    mkbd, mvbd = _block_diag_mem(_memkv(mem, g_mem, w_mem_kv), n_mem_heads)

    bsz = x.shape[0]
    n_streams = BATCH_STREAMS if bsz % BATCH_STREAMS == 0 else 1
    bs = bsz // n_streams
    u_pk = [_pack_rows(peer_u[l]) for l in range(depth)]
    v_pk = [_pack_rows(peer_v[l]) for l in range(depth)]
    states = [dict(x=x[s * bs:(s + 1) * bs], po=None, lo=s * bs, hi=(s + 1) * bs) for s in range(n_streams)]
    for layer in range(depth):
        for st in states:
            xs, po = st["x"], st["po"]
            mk, mv = mkbd[layer, st["lo"]:st["hi"]], mvbd[layer, st["lo"]:st["hi"]]
            if layer == n_a:
                st["kv"] = _kv_proj(xs, po, g_kv, w_kv, b_f)
            if layer < n_a:
                xs = _mix_a(xs, po, g_mix[layer], w_in_a[layer], conv_w[layer], mk, mv, w_out[layer])
            else:
                q, ym = _mix_b_in(xs, po, g_mix[layer], w_in_b[layer - n_a], mk, mv)
                y = _fox(q, *st["kv"])
                xs = _mix_b_out(xs, po, y, ym, w_out[layer])
            st["x"] = xs
            st["po"] = _peer(xs, g_ffn[layer], peer_wq[layer], peer_keys[layer], u_pk[layer], v_pk[layer])
    outs = [_final_norm(st["x"], st["po"], g_final) for st in states]
    return jnp.concatenate(outs, axis=0)
```

```python
import functools

import jax
import jax.numpy as jnp
from jax import lax
from jax.experimental import pallas as pl
from jax.experimental.pallas import tpu as pltpu
from jax.experimental.pallas import tpu_sc as plsc

EPS = 1e-6
HEAD_DIM = 64
CONV_K = 3
PEER_HEADS = 8
PEER_NKEYS = 128
PEER_HALF = 128
PEER_TOPK = 16
PEER_CHUNK = 128
LANES = 128
SUBLANES = 8
TOKEN_TILE = 512
ROUTE_TILE = 256
ATTN_TILE = 512
VMEM_LIMIT = 56 * 1024 * 1024
SC_CORES = 2
SC_SUBCORES = 16
SC_LANES = 16
SC_ROWS = 64
SC_GROUP = 16
TC_RETRIEVE_TILE = 128
TC_ROW_STRIDE = 136
BATCH_STREAMS = (("sc", 2), ("sc", 2), ("sc", 2), ("tc", 1), ("tc", 1))

BF16 = jnp.bfloat16
F32 = jnp.float32


def _params(*sem):
    return pltpu.CompilerParams(dimension_semantics=sem, vmem_limit_bytes=VMEM_LIMIT)


class _IssueOrder:
    last = None


_issue = _IssueOrder()


def _ordered_call(body, *, in_specs, **kw):
    n_in = len(in_specs)

    def run(*args):
        prev = _issue.last
        if prev is None:
            outs = pl.pallas_call(body, in_specs=in_specs, **kw)(*args)
        else:
            def with_token(*refs):
                body(*refs[:n_in], *refs[n_in + 1:])

            outs = pl.pallas_call(with_token, in_specs=list(in_specs) + [pl.BlockSpec(memory_space=pl.ANY)],
                                  **kw)(*args, prev)
        _issue.last = jax.tree.leaves(outs)[0]
        return outs

    return run


def _rms(x, g):
    return x * lax.rsqrt(jnp.mean(x * x, axis=-1, keepdims=True) + EPS) * g


def _mem_attend(qm, mkbd, mvbd, n_heads, n_mem):
    lg = jnp.dot(qm.astype(BF16), mkbd, preferred_element_type=F32) * (HEAD_DIM ** -0.5)
    parts = []
    for h in range(n_heads):
        l = lg[:, h * n_mem:(h + 1) * n_mem]
        e = jnp.exp(l - jnp.max(l, axis=-1, keepdims=True))
        parts.append((e / jnp.sum(e, axis=-1, keepdims=True)).astype(BF16))
    return jnp.dot(jnp.concatenate(parts, axis=1), mvbd, preferred_element_type=F32)


def _memkv_body(mem_ref, g_ref, w_ref, o_ref):
    mn = _rms(mem_ref[0], g_ref[0])
    o_ref[0, 0] = jnp.dot(mn.astype(BF16), w_ref[0], preferred_element_type=F32).astype(BF16)


def _memkv(mem, g_mem, w_mem_kv):
    bsz, n_mem, d = mem.shape
    depth, _, wd = w_mem_kv.shape
    return _ordered_call(
        _memkv_body,
        grid=(depth, bsz),
        in_specs=[
            pl.BlockSpec((1, n_mem, d), lambda l, b: (b, 0, 0)),
            pl.BlockSpec((1, 1, d), lambda l, b: (l, 0, 0)),
            pl.BlockSpec((1, d, wd), lambda l, b: (l, 0, 0)),
        ],
        out_specs=pl.BlockSpec((1, 1, n_mem, wd), lambda l, b: (l, b, 0, 0)),
        out_shape=jax.ShapeDtypeStruct((depth, bsz, n_mem, wd), BF16),
        compiler_params=_params("arbitrary", "arbitrary"),
        name="memkv",
    )(mem, g_mem.reshape(depth, 1, d), w_mem_kv.astype(BF16))


def _block_diag_mem(mkv, n_heads):
    depth, bsz, n_mem, _ = mkv.shape
    r = mkv.reshape(depth, bsz, n_mem, 2, n_heads, HEAD_DIM)
    eye = jnp.eye(n_heads, dtype=mkv.dtype)
    mk = jnp.einsum("lbmhd,hg->lbhdgm", r[:, :, :, 0], eye)
    mv = jnp.einsum("lbmhd,hg->lbhmgd", r[:, :, :, 1], eye)
    hw = n_heads * HEAD_DIM
    return mk.reshape(depth, bsz, hw, n_heads * n_mem), mv.reshape(depth, bsz, n_heads * n_mem, hw)


def _mix_a_body(has_po, conv_ch, n_mem_heads, n_mem, *refs):
    if has_po:
        x_ref, po_ref, g_ref, win_ref, cw_ref, mk_ref, mv_ref, wout_ref, o_ref, cbuf = refs
        x = x_ref[0] + po_ref[0]
    else:
        x_ref, g_ref, win_ref, cw_ref, mk_ref, mv_ref, wout_ref, o_ref, cbuf = refs
        x = x_ref[0]
    tm = x.shape[0]
    h = _rms(x, g_ref[...]).astype(BF16)
    z = jnp.dot(h, win_ref[...], preferred_element_type=F32)
    u = z[:, :conv_ch]
    c_gate = z[:, conv_ch:2 * conv_ch]
    b_gate = z[:, 2 * conv_ch:3 * conv_ch]
    qm = z[:, 3 * conv_ch:]
    cu = c_gate * u

    @pl.when(pl.program_id(1) == 0)
    def _():
        cbuf[0:SUBLANES, :] = jnp.zeros((SUBLANES, conv_ch), F32)

    cbuf[SUBLANES:SUBLANES + tm, :] = cu
    cu1 = cbuf[SUBLANES - 1:SUBLANES - 1 + tm, :]
    cu2 = cbuf[SUBLANES - 2:SUBLANES - 2 + tm, :]
    cw = cw_ref[...]
    y = b_gate * (cw[0:1] * cu2 + cw[1:2] * cu1 + cw[2:3] * cu)
    cbuf[0:SUBLANES, :] = cbuf[tm:tm + SUBLANES, :]

    ymem = _mem_attend(qm, mk_ref[0], mv_ref[0], n_mem_heads, n_mem)
    o_ref[0] = (x
                + jnp.dot(y.astype(BF16), wout_ref[0:conv_ch, :], preferred_element_type=F32)
                + jnp.dot(ymem.astype(BF16), wout_ref[conv_ch:, :], preferred_element_type=F32))


def _mix_a(x, po, g, w_in, conv_w, mkbd, mvbd, w_out):
    bsz, s_len, d = x.shape
    tm = min(TOKEN_TILE, s_len)
    conv_ch = conv_w.shape[1]
    mem_w = w_in.shape[1] - 3 * conv_ch
    n_mem_heads = mem_w // HEAD_DIM
    n_mem = mkbd.shape[2] // n_mem_heads
    tok = pl.BlockSpec((1, tm, d), lambda b, i: (b, i, 0))
    full = lambda a: pl.BlockSpec(a.shape, lambda b, i: (0,) * a.ndim)
    per_b = lambda a: pl.BlockSpec((1,) + a.shape[1:], lambda b, i: (b, 0, 0))
    g2 = g.reshape(1, d)
    win = w_in.astype(BF16)
    wout = w_out.astype(BF16)
    args = [x] + ([po] if po is not None else []) + [g2, win, conv_w, mkbd, mvbd, wout]
    specs = [tok] + ([tok] if po is not None else []) + [full(g2), full(win), full(conv_w), per_b(mkbd), per_b(mvbd), full(wout)]
    return _ordered_call(
        functools.partial(_mix_a_body, po is not None, conv_ch, n_mem_heads, n_mem),
        grid=(bsz, s_len // tm),
        in_specs=specs,
        out_specs=tok,
        out_shape=jax.ShapeDtypeStruct(x.shape, F32),
        scratch_shapes=[pltpu.VMEM((tm + SUBLANES, conv_ch), F32)],
        compiler_params=_params("arbitrary", "arbitrary"),
        name="mix_a",
    )(*args)


def _topk_rows(s, k, payload=None):
    n = s.shape[0]
    rows = lax.broadcasted_iota(jnp.int32, s.shape, 0)
    vals, outs = [], []
    for _ in range(k):
        m = jnp.max(s, axis=0, keepdims=True)
        pos = jnp.min(jnp.where(s == m, rows, n), axis=0, keepdims=True)
        sel = rows == pos
        vals.append(m)
        if payload is None:
            outs.append(pos)
        else:
            outs.append(jnp.sum(jnp.where(sel, payload, 0), axis=0, keepdims=True))
        s = jnp.where(sel, -jnp.inf, s)
    return jnp.concatenate(vals, axis=0), jnp.concatenate(outs, axis=0)


def _route_body(x_ref, g_ref, wq_ref, keys_ref, xn_ref, idx_ref, gate_ref, xnb):
    @pl.when(pl.program_id(1) == 0)
    def _():
        xn = _rms(x_ref[...], g_ref[...])
        xn_ref[...] = xn
        xnb[...] = xn.astype(BF16)

    q = jnp.dot(xnb[...], wq_ref[...], preferred_element_type=F32)
    tops = []
    for p in range(2):
        qp = q[:, p * PEER_HALF:(p + 1) * PEER_HALF].astype(BF16)
        st = lax.dot_general(keys_ref[0, p], qp, (((1,), (1,)), ((), ())),
                             preferred_element_type=F32)
        tops.append(_topk_rows(st, PEER_TOPK))
    (v1, i1), (v2, i2) = tops
    cand = jnp.concatenate([v1[i:i + 1] + v2 for i in range(PEER_TOPK)], axis=0)
    cidx = jnp.concatenate([i1[i:i + 1] * PEER_NKEYS + i2 for i in range(PEER_TOPK)], axis=0)
    top_s, idx = _topk_rows(cand, PEER_TOPK, payload=cidx)
    e = jnp.exp(top_s - top_s[0:1])
    idx_ref[...] = idx
    gate_ref[...] = e / jnp.sum(e, axis=0, keepdims=True)


def _peer_route(x2, g, w_q, keys):
    t, d = x2.shape
    tm = min(ROUTE_TILE, t)
    dk = 2 * PEER_HALF
    return _ordered_call(
        _route_body,
        grid=(t // tm, PEER_HEADS),
        in_specs=[
            pl.BlockSpec((tm, d), lambda i, h: (i, 0)),
            pl.BlockSpec((1, d), lambda i, h: (0, 0)),
            pl.BlockSpec((d, dk), lambda i, h: (0, h)),
            pl.BlockSpec((1, 2, PEER_NKEYS, PEER_HALF), lambda i, h: (h, 0, 0, 0)),
        ],
        out_specs=[
            pl.BlockSpec((tm, d), lambda i, h: (i, 0)),
            pl.BlockSpec((PEER_TOPK, tm), lambda i, h: (h, i)),
            pl.BlockSpec((PEER_TOPK, tm), lambda i, h: (h, i)),
        ],
        out_shape=[
            jax.ShapeDtypeStruct((t, d), F32),
            jax.ShapeDtypeStruct((PEER_HEADS * PEER_TOPK, t), jnp.int32),
            jax.ShapeDtypeStruct((PEER_HEADS * PEER_TOPK, t), F32),
        ],
        scratch_shapes=[pltpu.VMEM((tm, d), BF16)],
        compiler_params=_params("arbitrary", "arbitrary"),
        name="peer_route",
    )(x2, g.reshape(1, d), w_q.astype(BF16), keys.astype(BF16))


def _sc_mesh():
    return plsc.VectorSubcoreMesh(core_axis_name="c", subcore_axis_name="s",
                                  num_cores=SC_CORES, num_subcores=SC_SUBCORES)


def _sc_worker_id():
    return lax.axis_index("s") * SC_CORES + lax.axis_index("c")


def _sc_gather_loop(tab_hbm, idx_v, buf, sems, compute):
    per_tok = idx_v.shape[1] // SC_ROWS
    n_batches = SC_GROUP * per_tok

    def gather(j, slot):
        tok = j // per_tok
        row0 = pl.multiple_of((j % per_tok) * SC_ROWS, SC_ROWS)
        return tok, row0, pltpu.make_async_copy(
            tab_hbm.at[idx_v.at[tok, pl.ds(row0, SC_ROWS)]], buf.at[slot], sems.at[slot])

    nbuf = buf.shape[0]
    for b in range(nbuf - 1):
        gather(b, b)[2].start()

    def ring(jj, carry):
        for b in range(nbuf):
            j = jj * nbuf + b
            ahead = j + nbuf - 1

            @pl.when(ahead < n_batches)
            def _():
                gather(ahead, (b + nbuf - 1) % nbuf)[2].start()

            tok, row0, cp = gather(j, b)
            cp.wait()
            compute(tok, row0, buf.at[b])
        return carry

    lax.fori_loop(0, n_batches // nbuf, ring, 0)


def _pack_rows(tab):
    n, d = tab.shape
    bits = lax.bitcast_convert_type(tab.astype(BF16).reshape(n, d // (2 * SC_LANES), 2, SC_LANES), jnp.uint16)
    bits = bits.astype(jnp.uint32)
    words = (bits[:, :, 1, :] << 16) | bits[:, :, 0, :]
    return lax.bitcast_convert_type(words, jnp.int32).reshape(n, d // 2)


def _unpack_words(w):
    lo = lax.bitcast_convert_type(w << 16, F32)
    hi = lax.bitcast_convert_type(w & jnp.int32(-65536), F32)
    return lo, hi


def _sc_group_loop(n_groups, stage, work):
    for cp in stage(0, 0):
        cp.start()

    def group(g, carry):
        slot = g % 2
        for cp in stage(g, slot):
            cp.wait()

        @pl.when(g + 1 < n_groups)
        def _():
            for cp in stage(g + 1, 1 - slot):
                cp.start()

        work(g, slot)
        return carry

    lax.fori_loop(0, n_groups, group, 0)


def _sc_udot_body(groups_per_worker, u_hbm, idx_hbm, x_hbm, a_hbm, idx_v, x_v, a_v, buf, tr, sems, ssems):
    g0 = _sc_worker_id() * groups_per_worker
    lanes = lax.iota(jnp.int32, SC_LANES)
    chunks = x_v.shape[2] // (2 * SC_LANES)

    def stage(g, slot):
        gi = g0 + g
        return [pltpu.make_async_copy(idx_hbm.at[gi], idx_v.at[slot], ssems.at[slot, 0]),
                pltpu.make_async_copy(x_hbm.at[pl.ds(gi * SC_GROUP, SC_GROUP)], x_v.at[slot], ssems.at[slot, 1])]

    def work(g, slot):
        def compute(tok, row0, rows):
            for eb in range(SC_ROWS // SC_LANES):
                def body(c, accs):
                    x_lo = x_v[slot, tok, pl.ds(c * 2 * SC_LANES, SC_LANES)]
                    x_hi = x_v[slot, tok, pl.ds(c * 2 * SC_LANES + SC_LANES, SC_LANES)]
                    out = []
                    for e in range(SC_LANES):
                        lo, hi = _unpack_words(rows[eb * SC_LANES + e, pl.ds(c * SC_LANES, SC_LANES)])
                        out.append(accs[e] + lo * x_lo + hi * x_hi)
                    return tuple(out)

                accs = lax.fori_loop(0, chunks, body, tuple(jnp.zeros((SC_LANES,), F32) for _ in range(SC_LANES)))
                for e in range(SC_LANES):
                    tr[e, :] = accs[e]
                tot = jnp.zeros((SC_LANES,), F32)
                for c in range(SC_LANES):
                    tot = tot + plsc.load_gather(tr, [lanes, jnp.full((SC_LANES,), c, jnp.int32)])
                a_v[tok, pl.ds(row0 + eb * SC_LANES, SC_LANES)] = tot

        _sc_gather_loop(u_hbm, idx_v.at[slot], buf, sems, compute)
        pltpu.sync_copy(a_v, a_hbm.at[g0 + g])

    _sc_group_loop(groups_per_worker, stage, work)


def _sc_vsum_body(groups_per_worker, v_hbm, idx_hbm, w_hbm, o_hbm, idx_v, w_v, o_v, buf, sems, ssems):
    g0 = _sc_worker_id() * groups_per_worker
    d = o_v.shape[1]
    words_per_pass = SC_LANES // 2
    n_passes = d // (2 * SC_LANES * words_per_pass)

    def stage(g, slot):
        gi = g0 + g
        return [pltpu.make_async_copy(idx_hbm.at[gi], idx_v.at[slot], ssems.at[slot, 0]),
                pltpu.make_async_copy(w_hbm.at[gi], w_v.at[slot], ssems.at[slot, 1])]

    def work(g, slot):
        slot_v = jnp.full((SC_LANES,), slot, jnp.int32)

        def compute(tok, row0, rows):
            tok_v = jnp.full((SC_LANES,), tok, jnp.int32)
            for dg in range(n_passes):
                def body(r, accs):
                    wk = plsc.load_gather(w_v, [slot_v, tok_v, jnp.full((SC_LANES,), row0 + r, jnp.int32)])
                    out = []
                    for cc in range(words_per_pass):
                        lo, hi = _unpack_words(rows[r, pl.ds((dg * words_per_pass + cc) * SC_LANES, SC_LANES)])
                        out += [accs[2 * cc] + wk * lo, accs[2 * cc + 1] + wk * hi]
                    return tuple(out)

                accs = lax.fori_loop(0, SC_ROWS, body, tuple(jnp.zeros((SC_LANES,), F32) for _ in range(SC_LANES)))
                for cc in range(SC_LANES):
                    sl = (tok, pl.ds((dg * SC_LANES + cc) * SC_LANES, SC_LANES))
                    o_v[sl] = o_v[sl] + accs[cc]

        def zero(i, c):
            o_v[i // (d // SC_LANES), pl.ds((i % (d // SC_LANES)) * SC_LANES, SC_LANES)] = jnp.zeros((SC_LANES,), F32)
            return c

        lax.fori_loop(0, SC_GROUP * (d // SC_LANES), zero, 0)
        _sc_gather_loop(v_hbm, idx_v.at[slot], buf, sems, compute)
        pltpu.sync_copy(o_v, o_hbm.at[pl.ds((g0 + g) * SC_GROUP, SC_GROUP)])

    _sc_group_loop(groups_per_worker, stage, work)


def _gate_body(a_ref, g_ref, w_ref):
    a = a_ref[...]
    w_ref[...] = g_ref[...] * (0.5 * a * (1.0 + lax.erf(a * (2.0 ** -0.5))))


def _peer_retrieve(xn, idx, gate, u_tab, v_tab):
    t, d = xn.shape
    hk = idx.shape[1]
    n_groups = t // SC_GROUP
    groups_per_worker = n_groups // (SC_CORES * SC_SUBCORES)
    idx3 = idx.reshape(n_groups, SC_GROUP, hk)
    row_buf = pltpu.VMEM((2, SC_ROWS, d // 2), jnp.int32)
    sems = pltpu.SemaphoreType.DMA((2,))
    stage_sems = pltpu.SemaphoreType.DMA((2, 2))
    cost = pl.CostEstimate(flops=2 * t * hk * d, transcendentals=0,
                           bytes_accessed=2 * t * hk * d + 4 * (t * d + 2 * t * hk))

    a3 = pl.kernel(
        functools.partial(_sc_udot_body, groups_per_worker),
        out_type=jax.ShapeDtypeStruct((n_groups, SC_GROUP, hk), F32),
        mesh=_sc_mesh(),
        scratch_types=[pltpu.VMEM((2, SC_GROUP, hk), jnp.int32), pltpu.VMEM((2, SC_GROUP, d), F32),
                       pltpu.VMEM((SC_GROUP, hk), F32), row_buf, pltpu.VMEM((SC_LANES, SC_LANES), F32), sems,
                       stage_sems],
        compiler_params=pltpu.CompilerParams(needs_layout_passes=False),
        cost_estimate=cost,
        name="peer_udot",
    )(u_tab, idx3, xn)

    def finish():
        g3 = gate.reshape(n_groups, SC_GROUP, hk)
        gb = min(n_groups, 256)
        blk = pl.BlockSpec((gb, SC_GROUP, hk), lambda i: (i, 0, 0))
        w3 = _ordered_call(
            _gate_body, grid=(n_groups // gb,), in_specs=[blk, blk], out_specs=blk,
            out_shape=jax.ShapeDtypeStruct(a3.shape, F32), compiler_params=_params("arbitrary"), name="peer_gate",
        )(a3, g3)
        return pl.kernel(
            functools.partial(_sc_vsum_body, groups_per_worker),
            out_type=jax.ShapeDtypeStruct((t, d), F32),
            mesh=_sc_mesh(),
            scratch_types=[pltpu.VMEM((2, SC_GROUP, hk), jnp.int32), pltpu.VMEM((2, SC_GROUP, hk), F32),
                           pltpu.VMEM((SC_GROUP, d), F32), row_buf, sems, stage_sems],
            compiler_params=pltpu.CompilerParams(needs_layout_passes=False),
            cost_estimate=cost,
            name="peer_vsum",
        )(v_tab, idx3, w3)

    return finish


def _pack_rows_tc(tab):
    n, d = tab.shape
    bits = lax.bitcast_convert_type(tab.astype(BF16), jnp.uint16).astype(jnp.uint32)
    words = (bits[:, d // 2:] << 16) | bits[:, :d // 2]
    return lax.bitcast_convert_type(words, jnp.int32).reshape(n * (d // 2 // LANES), LANES)


def _tc_gather_rows(tab_ref, idx_ref, i, ubuf, n_rows, lines):
    for k in range(n_rows):
        r = idx_ref[i, k]
        ubuf[pl.ds(k, lines, stride=TC_ROW_STRIDE), :] = tab_ref[pl.ds(pl.multiple_of(r * lines, lines), lines), :]


def _tc_halves(ubuf, jb, n_rows):
    wv = ubuf[jb * TC_ROW_STRIDE:jb * TC_ROW_STRIDE + n_rows, :]
    lo = lax.bitcast_convert_type(wv << 16, F32)
    hi = lax.bitcast_convert_type(wv & jnp.int32(-65536), F32)
    return lo, hi


def _tc_udot_body(tab_ref, idx_ref, x_ref, o_ref, ubuf):
    tt, d = x_ref.shape
    n_rows = o_ref.shape[0]
    lines = d // 2 // LANES
    lane = lax.broadcasted_iota(jnp.int32, (n_rows, tt), 1)

    def token(i, out):
        _tc_gather_rows(tab_ref, idx_ref, i, ubuf, n_rows, lines)
        xrow = x_ref[pl.ds(i, 1), :]
        acc = jnp.zeros((n_rows, LANES), F32)
        for jb in range(lines):
            lo, hi = _tc_halves(ubuf, jb, n_rows)
            acc = (acc + lo * xrow[:, jb * LANES:(jb + 1) * LANES]
                   + hi * xrow[:, d // 2 + jb * LANES:d // 2 + (jb + 1) * LANES])
        return jnp.where(lane == i, jnp.sum(acc, axis=1, keepdims=True), out)

    o_ref[...] = lax.fori_loop(0, tt, token, jnp.zeros((n_rows, tt), F32))


def _tc_vsum_body(tab_ref, idx_ref, w_ref, o_ref, ubuf):
    tt, d = o_ref.shape
    n_rows = w_ref.shape[0]
    lines = d // 2 // LANES
    lane = lax.broadcasted_iota(jnp.int32, (n_rows, tt), 1)

    def group(g, carry):
        rows = []
        for j in range(SUBLANES):
            i = g * SUBLANES + j
            _tc_gather_rows(tab_ref, idx_ref, i, ubuf, n_rows, lines)
            col = jnp.sum(jnp.where(lane == i, w_ref[...], 0.0), axis=1, keepdims=True)
            halves = [_tc_halves(ubuf, jb, n_rows) for jb in range(lines)]
            rows.append(jnp.concatenate(
                [jnp.sum(lo * col, axis=0, keepdims=True) for lo, _ in halves]
                + [jnp.sum(hi * col, axis=0, keepdims=True) for _, hi in halves], axis=1))
        o_ref[pl.ds(pl.multiple_of(g * SUBLANES, SUBLANES), SUBLANES), :] = jnp.concatenate(rows, axis=0)
        return carry

    lax.fori_loop(0, tt // SUBLANES, group, 0)


def _tc_table_specs(tab4, hk, tt):
    lines_total = tab4.shape[0]
    return [pl.BlockSpec((lines_total, LANES), lambda i: (0, 0), pipeline_mode=pl.Buffered(1)),
            pl.BlockSpec((tt, hk), lambda i: (i, 0), memory_space=pltpu.SMEM)]


def _tc_udot(tab4, idx, x):
    t, d = x.shape
    hk = idx.shape[1]
    tt = min(TC_RETRIEVE_TILE, t)
    return _ordered_call(
        _tc_udot_body, grid=(t // tt,),
        in_specs=_tc_table_specs(tab4, hk, tt) + [pl.BlockSpec((tt, d), lambda i: (i, 0))],
        out_specs=pl.BlockSpec((hk, tt), lambda i: (0, i)),
        out_shape=jax.ShapeDtypeStruct((hk, t), F32),
        scratch_shapes=[pltpu.VMEM((d // 2 // LANES * TC_ROW_STRIDE, LANES), jnp.int32)],
        compiler_params=_params("arbitrary"), name="tc_udot",
    )(tab4, idx, x)


def _tc_vsum(tab4, idx, w_t):
    hk, t = w_t.shape
    d = tab4.shape[1] * 2 * (tab4.shape[0] // PEER_NKEYS ** 2)
    tt = min(TC_RETRIEVE_TILE, t)
    return _ordered_call(
        _tc_vsum_body, grid=(t // tt,),
        in_specs=_tc_table_specs(tab4, hk, tt) + [pl.BlockSpec((hk, tt), lambda i: (0, i))],
        out_specs=pl.BlockSpec((tt, d), lambda i: (i, 0)),
        out_shape=jax.ShapeDtypeStruct((t, d), F32),
        scratch_shapes=[pltpu.VMEM((d // 2 // LANES * TC_ROW_STRIDE, LANES), jnp.int32)],
        compiler_params=_params("arbitrary"), name="tc_vsum",
    )(tab4, idx, w_t)


def _peer_retrieve_tc(xn, idx_t, gate_t, u_tab4, v_tab4):
    hk, t = idx_t.shape
    idx = idx_t.T
    a_t = _tc_udot(u_tab4, idx, xn)

    def finish():
        tt = min(TC_RETRIEVE_TILE, t)
        blk = pl.BlockSpec((hk, tt), lambda i: (0, i))
        w_t = _ordered_call(
            _gate_body, grid=(t // tt,), in_specs=[blk, blk], out_specs=blk,
            out_shape=jax.ShapeDtypeStruct((hk, t), F32), compiler_params=_params("arbitrary"), name="tc_gate",
        )(a_t, gate_t)
        return _tc_vsum(v_tab4, idx, w_t)

    return finish


def _peer_start(x, g, w_q, keys, tables, on_tc):
    bsz, s_len, d = x.shape
    xn, idx_t, gate_t = _peer_route(x.reshape(bsz * s_len, d), g, w_q, keys)
    if on_tc:
        fin = _peer_retrieve_tc(xn, idx_t, gate_t, tables[2], tables[3])
    else:
        fin = _peer_retrieve(xn, idx_t.T, gate_t.T, tables[0], tables[1])
    return lambda: fin().reshape(bsz, s_len, d)


def _kv_body(fox_w, x_ref, po_ref, g_ref, w_ref, bf_ref, k_ref, v_ref, c_ref, ct_ref, carry):
    x = x_ref[0] + po_ref[0]
    tm = x.shape[0]
    hs = _rms(x, g_ref[...]).astype(BF16)
    z = jnp.dot(hs, w_ref[...], preferred_element_type=F32)
    k_ref[0] = z[:, :fox_w].astype(BF16)
    v_ref[0] = z[:, fox_w:2 * fox_w].astype(BF16)
    zf = z[:, 2 * fox_w:] + bf_ref[...]
    logf = jnp.minimum(zf, 0.0) - jnp.log1p(jnp.exp(-jnp.abs(zf)))

    @pl.when(pl.program_id(1) == 0)
    def _():
        carry[...] = jnp.zeros(carry.shape, F32)

    r = lax.broadcasted_iota(jnp.int32, (tm, tm), 0)
    c = lax.broadcasted_iota(jnp.int32, (tm, tm), 1)
    tri = (c <= r).astype(F32)
    cs = jnp.dot(tri, logf, preferred_element_type=F32, precision=lax.Precision.HIGHEST) + carry[...]
    carry[...] = cs[tm - 1:tm, :]
    c_ref[0] = cs
    ct_ref[0] = cs.T[0:ct_ref.shape[1], :]


def _kv_proj(x, po, g, w_kv, b_f):
    bsz, s_len, d = x.shape
    tm = min(TOKEN_TILE, s_len)
    n_heads = b_f.shape[0]
    fox_w = (w_kv.shape[1] - n_heads) // 2
    w = jnp.pad(w_kv, ((0, 0), (0, LANES - n_heads))).astype(BF16)
    bf = jnp.pad(b_f, (0, LANES - n_heads)).reshape(1, LANES)
    hp = -(-n_heads // SUBLANES) * SUBLANES
    tok = lambda w_, dt: (pl.BlockSpec((1, tm, w_), lambda b, i: (b, i, 0)), jax.ShapeDtypeStruct((bsz, s_len, w_), dt))
    (ks, ksh), (vs, vsh), (cs, csh) = tok(fox_w, BF16), tok(fox_w, BF16), tok(LANES, F32)
    return _ordered_call(
        functools.partial(_kv_body, fox_w),
        grid=(bsz, s_len // tm),
        in_specs=[
            pl.BlockSpec((1, tm, d), lambda b, i: (b, i, 0)),
            pl.BlockSpec((1, tm, d), lambda b, i: (b, i, 0)),
            pl.BlockSpec((1, d), lambda b, i: (0, 0)),
            pl.BlockSpec(w.shape, lambda b, i: (0, 0)),
            pl.BlockSpec((1, LANES), lambda b, i: (0, 0)),
        ],
        out_specs=[ks, vs, cs, pl.BlockSpec((1, hp, tm), lambda b, i: (b, 0, i))],
        out_shape=[ksh, vsh, csh, jax.ShapeDtypeStruct((bsz, hp, s_len), F32)],
        scratch_shapes=[pltpu.VMEM((1, LANES), F32)],
        compiler_params=_params("arbitrary", "arbitrary"),
        name="kv_proj",
    )(x, po, g.reshape(1, d), w, bf)


def _mix_b_in_body(fox_w, n_mem_heads, n_mem, x_ref, po_ref, g_ref, win_ref, mk_ref, mv_ref, q_ref, ym_ref):
    x = x_ref[0] + po_ref[0]
    h = _rms(x, g_ref[...]).astype(BF16)
    z = jnp.dot(h, win_ref[...], preferred_element_type=F32)
    q_ref[0] = (z[:, :fox_w] * (HEAD_DIM ** -0.5)).astype(BF16)
    ym_ref[0] = _mem_attend(z[:, fox_w:], mk_ref[0], mv_ref[0], n_mem_heads, n_mem).astype(BF16)


def _mix_b_in(x, po, g, w_in, mkbd, mvbd):
    bsz, s_len, d = x.shape
    tm = min(TOKEN_TILE, s_len)
    mem_w = mvbd.shape[2]
    fox_w = w_in.shape[1] - mem_w
    n_mem_heads = mem_w // HEAD_DIM
    n_mem = mkbd.shape[2] // n_mem_heads
    win = w_in.astype(BF16)
    tok = lambda w_: pl.BlockSpec((1, tm, w_), lambda b, i: (b, i, 0))
    per_b = lambda a: pl.BlockSpec((1,) + a.shape[1:], lambda b, i: (b, 0, 0))
    return _ordered_call(
        functools.partial(_mix_b_in_body, fox_w, n_mem_heads, n_mem),
        grid=(bsz, s_len // tm),
        in_specs=[tok(d), tok(d), pl.BlockSpec((1, d), lambda b, i: (0, 0)),
                  pl.BlockSpec(win.shape, lambda b, i: (0, 0)), per_b(mkbd), per_b(mvbd)],
        out_specs=[tok(fox_w), tok(mem_w)],
        out_shape=[jax.ShapeDtypeStruct((bsz, s_len, fox_w), BF16),
                   jax.ShapeDtypeStruct((bsz, s_len, mem_w), BF16)],
        compiler_params=_params("arbitrary", "arbitrary"),
        name="mix_b_in",
    )(x, po, g.reshape(1, d), win, mkbd, mvbd)


def _fox_body(n_heads, q_ref, k_ref, v_ref, c_ref, ct_ref, o_ref, m_scr, l_scr, acc):
    qi, ki = pl.program_id(1), pl.program_id(2)
    tq, tk = q_ref.shape[1], k_ref.shape[1]

    @pl.when(ki == 0)
    def _():
        m_scr[...] = jnp.full(m_scr.shape, -jnp.inf, F32)
        l_scr[...] = jnp.zeros(l_scr.shape, F32)
        acc[...] = jnp.zeros(acc.shape, F32)

    def step(masked):
        if masked:
            keep = (lax.broadcasted_iota(jnp.int32, (tq, tk), 0)
                    >= lax.broadcasted_iota(jnp.int32, (tq, tk), 1))
        for h in range(n_heads):
            sl = slice(h * HEAD_DIM, (h + 1) * HEAD_DIM)
            s = lax.dot_general(q_ref[0, :, sl], k_ref[0, :, sl], (((1,), (1,)), ((), ())),
                                preferred_element_type=F32)
            s = s + (c_ref[0, :, h:h + 1] - ct_ref[0, h:h + 1, :])
            if masked:
                s = jnp.where(keep, s, -jnp.inf)
            m_old = m_scr[:, h:h + 1]
            m_new = jnp.maximum(m_old, jnp.max(s, axis=-1, keepdims=True))
            alpha = jnp.exp(m_old - m_new)
            p = jnp.exp(s - m_new)
            l_scr[:, h:h + 1] = alpha * l_scr[:, h:h + 1] + jnp.sum(p, axis=-1, keepdims=True)
            acc[:, sl] = alpha * acc[:, sl] + jnp.dot(p.astype(BF16), v_ref[0, :, sl],
                                                      preferred_element_type=F32)
            m_scr[:, h:h + 1] = m_new

    @pl.when(ki < qi)
    def _():
        step(False)

    @pl.when(ki == qi)
    def _():
        step(True)
        for h in range(n_heads):
            sl = slice(h * HEAD_DIM, (h + 1) * HEAD_DIM)
            o_ref[0, :, sl] = (acc[:, sl] / l_scr[:, h:h + 1]).astype(BF16)


def _fox(q, k, v, c, ct):
    bsz, s_len, w = q.shape
    n_heads = w // HEAD_DIM
    t = min(ATTN_TILE, s_len)
    nb = s_len // t
    return _ordered_call(
        functools.partial(_fox_body, n_heads),
        grid=(bsz, nb, nb),
        in_specs=[
            pl.BlockSpec((1, t, w), lambda b, i, j: (b, i, 0)),
            pl.BlockSpec((1, t, w), lambda b, i, j: (b, jnp.minimum(i, j), 0)),
            pl.BlockSpec((1, t, w), lambda b, i, j: (b, jnp.minimum(i, j), 0)),
            pl.BlockSpec((1, t, c.shape[2]), lambda b, i, j: (b, i, 0)),
            pl.BlockSpec((1, ct.shape[1], t), lambda b, i, j: (b, 0, jnp.minimum(i, j))),
        ],
        out_specs=pl.BlockSpec((1, t, w), lambda b, i, j: (b, i, 0)),
        out_shape=jax.ShapeDtypeStruct((bsz, s_len, w), BF16),
        scratch_shapes=[pltpu.VMEM((t, LANES), F32), pltpu.VMEM((t, LANES), F32), pltpu.VMEM((t, w), F32)],
        compiler_params=_params("arbitrary", "arbitrary", "arbitrary"),
        name="fox",
    )(q, k, v, c, ct)


def _mix_b_out_body(fox_w, x_ref, po_ref, y_ref, ym_ref, wout_ref, o_ref):
    o_ref[0] = (x_ref[0] + po_ref[0]
                + jnp.dot(y_ref[0], wout_ref[0:fox_w, :], preferred_element_type=F32)
                + jnp.dot(ym_ref[0], wout_ref[fox_w:, :], preferred_element_type=F32))


def _mix_b_out(x, po, y, ym, w_out):
    bsz, s_len, d = x.shape
    tm = min(TOKEN_TILE, s_len)
    fox_w = y.shape[2]
    wout = w_out.astype(BF16)
    tok = lambda w_: pl.BlockSpec((1, tm, w_), lambda b, i: (b, i, 0))
    return _ordered_call(
        functools.partial(_mix_b_out_body, fox_w),
        grid=(bsz, s_len // tm),
        in_specs=[tok(d), tok(d), tok(fox_w), tok(ym.shape[2]), pl.BlockSpec(wout.shape, lambda b, i: (0, 0))],
        out_specs=tok(d),
        out_shape=jax.ShapeDtypeStruct(x.shape, F32),
        compiler_params=_params("arbitrary", "arbitrary"),
        name="mix_b_out",
    )(x, po, y, ym, wout)


def _final_body(x_ref, po_ref, g_ref, o_ref):
    o_ref[0] = _rms(x_ref[0] + po_ref[0], g_ref[...])


def _final_norm(x, po, g):
    bsz, s_len, d = x.shape
    tm = min(TOKEN_TILE, s_len)
    tok = pl.BlockSpec((1, tm, d), lambda b, i: (b, i, 0))
    return _ordered_call(
        _final_body,
        grid=(bsz, s_len // tm),
        in_specs=[tok, tok, pl.BlockSpec((1, d), lambda b, i: (0, 0))],
        out_specs=tok,
        out_shape=jax.ShapeDtypeStruct(x.shape, F32),
        compiler_params=_params("arbitrary", "arbitrary"),
        name="final_norm",
    )(x, po, g.reshape(1, d))


def kernel(x, mem, g_mix, w_in_a, conv_w, w_in_b, w_out, g_mem, w_mem_kv, g_kv, w_kv, b_f, g_ffn,
           peer_wq, peer_keys, peer_u, peer_v, g_final):
    depth = g_mix.shape[0]
    n_a = w_in_a.shape[0]
    n_mem_heads = w_mem_kv.shape[2] // (2 * HEAD_DIM)
    mkbd, mvbd = _block_diag_mem(_memkv(mem, g_mem, w_mem_kv), n_mem_heads)

    bsz = x.shape[0]
    unit = bsz // sum(r for _, r in BATCH_STREAMS)
    plan = [(kind == "tc", r * unit) for kind, r in BATCH_STREAMS] if unit * sum(
        r for _, r in BATCH_STREAMS) == bsz else [(False, bsz)]
    tables = [(_pack_rows(peer_u[l]), _pack_rows(peer_v[l]),
               _pack_rows_tc(peer_u[l]), _pack_rows_tc(peer_v[l])) for l in range(depth)]
    states, lo = [], 0
    for on_tc, rows in plan:
        states.append(dict(x=x[lo:lo + rows], po=None, lo=lo, hi=lo + rows, on_tc=on_tc))
        lo += rows
    _issue.last = None

    def front(st, layer):
        xs, po = st["x"], st["po"]
        mk, mv = mkbd[layer, st["lo"]:st["hi"]], mvbd[layer, st["lo"]:st["hi"]]
        if layer == n_a:
            st["kv"] = _kv_proj(xs, po, g_kv, w_kv, b_f)
        if layer < n_a:
            xs = _mix_a(xs, po, g_mix[layer], w_in_a[layer], conv_w[layer], mk, mv, w_out[layer])
        else:
            q, ym = _mix_b_in(xs, po, g_mix[layer], w_in_b[layer - n_a], mk, mv)
            y = _fox(q, *st["kv"])
            xs = _mix_b_out(xs, po, y, ym, w_out[layer])
        st["x"] = xs
        st["fin"] = _peer_start(xs, g_ffn[layer], peer_wq[layer], peer_keys[layer], tables[layer], st["on_tc"])

    def back(st):
        st["po"] = st["fin"]()

    for layer in range(depth):
        waiting = [st for st in states if not st["on_tc"]]
        for st in waiting:
            front(st, layer)
        for st in states:
            if st["on_tc"]:
                front(st, layer)
                if waiting:
                    back(waiting.pop(0))
                back(st)
                if waiting:
                    back(waiting.pop(0))
        for st in waiting:
            back(st)
    outs = [_final_norm(st["x"], st["po"], g_final) for st in states]
    return jnp.concatenate(outs, axis=0)
```

```python
import functools

import jax
import jax.numpy as jnp
from jax import lax
from jax.experimental import pallas as pl
from jax.experimental.pallas import tpu as pltpu
from jax.experimental.pallas import tpu_sc as plsc

EPS = 1e-6
HEAD_DIM = 64
CONV_K = 3
PEER_HEADS = 8
PEER_NKEYS = 128
PEER_HALF = 128
PEER_TOPK = 16
PEER_CHUNK = 128
LANES = 128
SUBLANES = 8
TOKEN_TILE = 512
ROUTE_TILE = 256
ATTN_TILE = 512
VMEM_LIMIT = 56 * 1024 * 1024
SC_CORES = 2
SC_SUBCORES = 16
SC_LANES = 16
SC_ROWS = 64
SC_GROUP = 16
TC_RETRIEVE_TILE = 128
TC_ROW_STRIDE = 136
BATCH_STREAMS = ((2, "sc", "sc"), (2, "sc", "sc"), (1, "sc", "sc"), (1, "tc", "sc"), (1, "tc", "tc"), (1, "tc", "tc"))

BF16 = jnp.bfloat16
F32 = jnp.float32


def _params(*sem):
    return pltpu.CompilerParams(dimension_semantics=sem, vmem_limit_bytes=VMEM_LIMIT)


class _IssueOrder:
    last = None


_issue = _IssueOrder()


def _ordered_call(body, *, in_specs, **kw):
    n_in = len(in_specs)

    def run(*args):
        prev = _issue.last
        if prev is None:
            outs = pl.pallas_call(body, in_specs=in_specs, **kw)(*args)
        else:
            def with_token(*refs):
                body(*refs[:n_in], *refs[n_in + 1:])

            outs = pl.pallas_call(with_token, in_specs=list(in_specs) + [pl.BlockSpec(memory_space=pl.ANY)],
                                  **kw)(*args, prev)
        _issue.last = jax.tree.leaves(outs)[0]
        return outs

    return run


def _rms(x, g):
    return x * lax.rsqrt(jnp.mean(x * x, axis=-1, keepdims=True) + EPS) * g


def _mem_attend(qm, mkbd, mvbd, n_heads, n_mem):
    lg = jnp.dot(qm.astype(BF16), mkbd, preferred_element_type=F32) * (HEAD_DIM ** -0.5)
    parts = []
    for h in range(n_heads):
        l = lg[:, h * n_mem:(h + 1) * n_mem]
        e = jnp.exp(l - jnp.max(l, axis=-1, keepdims=True))
        parts.append((e / jnp.sum(e, axis=-1, keepdims=True)).astype(BF16))
    return jnp.dot(jnp.concatenate(parts, axis=1), mvbd, preferred_element_type=F32)


def _memkv_body(mem_ref, g_ref, w_ref, o_ref):
    mn = _rms(mem_ref[0], g_ref[0])
    o_ref[0, 0] = jnp.dot(mn.astype(BF16), w_ref[0], preferred_element_type=F32).astype(BF16)


def _memkv(mem, g_mem, w_mem_kv):
    bsz, n_mem, d = mem.shape
    depth, _, wd = w_mem_kv.shape
    return _ordered_call(
        _memkv_body,
        grid=(depth, bsz),
        in_specs=[
            pl.BlockSpec((1, n_mem, d), lambda l, b: (b, 0, 0)),
            pl.BlockSpec((1, 1, d), lambda l, b: (l, 0, 0)),
            pl.BlockSpec((1, d, wd), lambda l, b: (l, 0, 0)),
        ],
        out_specs=pl.BlockSpec((1, 1, n_mem, wd), lambda l, b: (l, b, 0, 0)),
        out_shape=jax.ShapeDtypeStruct((depth, bsz, n_mem, wd), BF16),
        compiler_params=_params("arbitrary", "arbitrary"),
        name="memkv",
    )(mem, g_mem.reshape(depth, 1, d), w_mem_kv.astype(BF16))


def _block_diag_mem(mkv, n_heads):
    depth, bsz, n_mem, _ = mkv.shape
    r = mkv.reshape(depth, bsz, n_mem, 2, n_heads, HEAD_DIM)
    eye = jnp.eye(n_heads, dtype=mkv.dtype)
    mk = jnp.einsum("lbmhd,hg->lbhdgm", r[:, :, :, 0], eye)
    mv = jnp.einsum("lbmhd,hg->lbhmgd", r[:, :, :, 1], eye)
    hw = n_heads * HEAD_DIM
    return mk.reshape(depth, bsz, hw, n_heads * n_mem), mv.reshape(depth, bsz, n_heads * n_mem, hw)


def _mix_a_body(has_po, conv_ch, n_mem_heads, n_mem, *refs):
    if has_po:
        x_ref, po_ref, g_ref, win_ref, cw_ref, mk_ref, mv_ref, wout_ref, o_ref, cbuf = refs
        x = x_ref[0] + po_ref[0]
    else:
        x_ref, g_ref, win_ref, cw_ref, mk_ref, mv_ref, wout_ref, o_ref, cbuf = refs
        x = x_ref[0]
    tm = x.shape[0]
    h = _rms(x, g_ref[...]).astype(BF16)
    z = jnp.dot(h, win_ref[...], preferred_element_type=F32)
    u = z[:, :conv_ch]
    c_gate = z[:, conv_ch:2 * conv_ch]
    b_gate = z[:, 2 * conv_ch:3 * conv_ch]
    qm = z[:, 3 * conv_ch:]
    cu = c_gate * u

    @pl.when(pl.program_id(1) == 0)
    def _():
        cbuf[0:SUBLANES, :] = jnp.zeros((SUBLANES, conv_ch), F32)

    cbuf[SUBLANES:SUBLANES + tm, :] = cu
    cu1 = cbuf[SUBLANES - 1:SUBLANES - 1 + tm, :]
    cu2 = cbuf[SUBLANES - 2:SUBLANES - 2 + tm, :]
    cw = cw_ref[...]
    y = b_gate * (cw[0:1] * cu2 + cw[1:2] * cu1 + cw[2:3] * cu)
    cbuf[0:SUBLANES, :] = cbuf[tm:tm + SUBLANES, :]

    ymem = _mem_attend(qm, mk_ref[0], mv_ref[0], n_mem_heads, n_mem)
    o_ref[0] = (x
                + jnp.dot(y.astype(BF16), wout_ref[0:conv_ch, :], preferred_element_type=F32)
                + jnp.dot(ymem.astype(BF16), wout_ref[conv_ch:, :], preferred_element_type=F32))


def _mix_a(x, po, g, w_in, conv_w, mkbd, mvbd, w_out):
    bsz, s_len, d = x.shape
    tm = min(TOKEN_TILE, s_len)
    conv_ch = conv_w.shape[1]
    mem_w = w_in.shape[1] - 3 * conv_ch
    n_mem_heads = mem_w // HEAD_DIM
    n_mem = mkbd.shape[2] // n_mem_heads
    tok = pl.BlockSpec((1, tm, d), lambda b, i: (b, i, 0))
    full = lambda a: pl.BlockSpec(a.shape, lambda b, i: (0,) * a.ndim)
    per_b = lambda a: pl.BlockSpec((1,) + a.shape[1:], lambda b, i: (b, 0, 0))
    g2 = g.reshape(1, d)
    win = w_in.astype(BF16)
    wout = w_out.astype(BF16)
    args = [x] + ([po] if po is not None else []) + [g2, win, conv_w, mkbd, mvbd, wout]
    specs = [tok] + ([tok] if po is not None else []) + [full(g2), full(win), full(conv_w), per_b(mkbd), per_b(mvbd), full(wout)]
    return _ordered_call(
        functools.partial(_mix_a_body, po is not None, conv_ch, n_mem_heads, n_mem),
        grid=(bsz, s_len // tm),
        in_specs=specs,
        out_specs=tok,
        out_shape=jax.ShapeDtypeStruct(x.shape, F32),
        scratch_shapes=[pltpu.VMEM((tm + SUBLANES, conv_ch), F32)],
        compiler_params=_params("arbitrary", "arbitrary"),
        name="mix_a",
    )(*args)


def _topk_rows(s, k, payload=None):
    n = s.shape[0]
    rows = lax.broadcasted_iota(jnp.int32, s.shape, 0)
    vals, outs = [], []
    for _ in range(k):
        m = jnp.max(s, axis=0, keepdims=True)
        pos = jnp.min(jnp.where(s == m, rows, n), axis=0, keepdims=True)
        sel = rows == pos
        vals.append(m)
        if payload is None:
            outs.append(pos)
        else:
            outs.append(jnp.sum(jnp.where(sel, payload, 0), axis=0, keepdims=True))
        s = jnp.where(sel, -jnp.inf, s)
    return jnp.concatenate(vals, axis=0), jnp.concatenate(outs, axis=0)


def _route_body(x_ref, g_ref, wq_ref, keys_ref, xn_ref, idx_ref, gate_ref, xnb):
    @pl.when(pl.program_id(1) == 0)
    def _():
        xn = _rms(x_ref[...], g_ref[...])
        xn_ref[...] = xn
        xnb[...] = xn.astype(BF16)

    q = jnp.dot(xnb[...], wq_ref[...], preferred_element_type=F32)
    tops = []
    for p in range(2):
        qp = q[:, p * PEER_HALF:(p + 1) * PEER_HALF].astype(BF16)
        st = lax.dot_general(keys_ref[0, p], qp, (((1,), (1,)), ((), ())),
                             preferred_element_type=F32)
        tops.append(_topk_rows(st, PEER_TOPK))
    (v1, i1), (v2, i2) = tops
    cand, cidx, i = [], [], 0
    while i < PEER_TOPK:
        n_j = PEER_TOPK // (i + 1)
        if n_j > 1:
            n_j = min(PEER_TOPK, -(-n_j // SUBLANES) * SUBLANES)
            cand.append(v1[i:i + 1] + v2[0:n_j])
            cidx.append(i1[i:i + 1] * PEER_NKEYS + i2[0:n_j])
            i += 1
        else:
            cand.append(v1[i:] + v2[0:1])
            cidx.append(i1[i:] * PEER_NKEYS + i2[0:1])
            i = PEER_TOPK
    top_s, idx = _topk_rows(jnp.concatenate(cand, axis=0), PEER_TOPK, payload=jnp.concatenate(cidx, axis=0))
    e = jnp.exp(top_s - top_s[0:1])
    idx_ref[...] = idx
    gate_ref[...] = e / jnp.sum(e, axis=0, keepdims=True)


def _peer_route(x2, g, w_q, keys):
    t, d = x2.shape
    tm = min(ROUTE_TILE, t)
    dk = 2 * PEER_HALF
    return _ordered_call(
        _route_body,
        grid=(t // tm, PEER_HEADS),
        in_specs=[
            pl.BlockSpec((tm, d), lambda i, h: (i, 0)),
            pl.BlockSpec((1, d), lambda i, h: (0, 0)),
            pl.BlockSpec((d, dk), lambda i, h: (0, h)),
            pl.BlockSpec((1, 2, PEER_NKEYS, PEER_HALF), lambda i, h: (h, 0, 0, 0)),
        ],
        out_specs=[
            pl.BlockSpec((tm, d), lambda i, h: (i, 0)),
            pl.BlockSpec((PEER_TOPK, tm), lambda i, h: (h, i)),
            pl.BlockSpec((PEER_TOPK, tm), lambda i, h: (h, i)),
        ],
        out_shape=[
            jax.ShapeDtypeStruct((t, d), F32),
            jax.ShapeDtypeStruct((PEER_HEADS * PEER_TOPK, t), jnp.int32),
            jax.ShapeDtypeStruct((PEER_HEADS * PEER_TOPK, t), F32),
        ],
        scratch_shapes=[pltpu.VMEM((tm, d), BF16)],
        compiler_params=_params("arbitrary", "arbitrary"),
        name="peer_route",
    )(x2, g.reshape(1, d), w_q.astype(BF16), keys.astype(BF16))


def _sc_mesh():
    return plsc.VectorSubcoreMesh(core_axis_name="c", subcore_axis_name="s",
                                  num_cores=SC_CORES, num_subcores=SC_SUBCORES)


def _sc_worker_id():
    return lax.axis_index("s") * SC_CORES + lax.axis_index("c")


def _sc_gather_loop(tab_hbm, idx_v, buf, sems, compute):
    per_tok = idx_v.shape[1] // SC_ROWS
    n_batches = SC_GROUP * per_tok

    def gather(j, slot):
        tok = j // per_tok
        row0 = pl.multiple_of((j % per_tok) * SC_ROWS, SC_ROWS)
        return tok, row0, pltpu.make_async_copy(
            tab_hbm.at[idx_v.at[tok, pl.ds(row0, SC_ROWS)]], buf.at[slot], sems.at[slot])

    nbuf = buf.shape[0]
    for b in range(nbuf - 1):
        gather(b, b)[2].start()

    def ring(jj, carry):
        for b in range(nbuf):
            j = jj * nbuf + b
            ahead = j + nbuf - 1

            @pl.when(ahead < n_batches)
            def _():
                gather(ahead, (b + nbuf - 1) % nbuf)[2].start()

            tok, row0, cp = gather(j, b)
            cp.wait()
            compute(tok, row0, buf.at[b])
        return carry

    lax.fori_loop(0, n_batches // nbuf, ring, 0)


def _pack_rows(tab):
    n, d = tab.shape
    bits = lax.bitcast_convert_type(tab.astype(BF16).reshape(n, d // (2 * SC_LANES), 2, SC_LANES), jnp.uint16)
    bits = bits.astype(jnp.uint32)
    words = (bits[:, :, 1, :] << 16) | bits[:, :, 0, :]
    return lax.bitcast_convert_type(words, jnp.int32).reshape(n, d // 2)


def _unpack_words(w):
    lo = lax.bitcast_convert_type(w << 16, F32)
    hi = lax.bitcast_convert_type(w & jnp.int32(-65536), F32)
    return lo, hi


def _sc_group_loop(n_groups, stage, work):
    for cp in stage(0, 0):
        cp.start()

    def group(g, carry):
        slot = g % 2
        for cp in stage(g, slot):
            cp.wait()

        @pl.when(g + 1 < n_groups)
        def _():
            for cp in stage(g + 1, 1 - slot):
                cp.start()

        work(g, slot)
        return carry

    lax.fori_loop(0, n_groups, group, 0)


def _sc_udot_body(groups_per_worker, u_hbm, idx_hbm, x_hbm, a_hbm, idx_v, x_v, a_v, buf, tr, sems, ssems):
    g0 = _sc_worker_id() * groups_per_worker
    lanes = lax.iota(jnp.int32, SC_LANES)
    chunks = x_v.shape[2] // (2 * SC_LANES)

    def stage(g, slot):
        gi = g0 + g
        return [pltpu.make_async_copy(idx_hbm.at[gi], idx_v.at[slot], ssems.at[slot, 0]),
                pltpu.make_async_copy(x_hbm.at[pl.ds(gi * SC_GROUP, SC_GROUP)], x_v.at[slot], ssems.at[slot, 1])]

    def work(g, slot):
        def compute(tok, row0, rows):
            for eb in range(SC_ROWS // SC_LANES):
                def body(c, accs):
                    x_lo = x_v[slot, tok, pl.ds(c * 2 * SC_LANES, SC_LANES)]
                    x_hi = x_v[slot, tok, pl.ds(c * 2 * SC_LANES + SC_LANES, SC_LANES)]
                    out = []
                    for e in range(SC_LANES):
                        lo, hi = _unpack_words(rows[eb * SC_LANES + e, pl.ds(c * SC_LANES, SC_LANES)])
                        out.append(accs[e] + lo * x_lo + hi * x_hi)
                    return tuple(out)

                accs = lax.fori_loop(0, chunks, body, tuple(jnp.zeros((SC_LANES,), F32) for _ in range(SC_LANES)))
                for e in range(SC_LANES):
                    tr[e, :] = accs[e]
                tot = jnp.zeros((SC_LANES,), F32)
                for c in range(SC_LANES):
                    tot = tot + plsc.load_gather(tr, [lanes, jnp.full((SC_LANES,), c, jnp.int32)])
                a_v[tok, pl.ds(row0 + eb * SC_LANES, SC_LANES)] = tot

        _sc_gather_loop(u_hbm, idx_v.at[slot], buf, sems, compute)
        pltpu.sync_copy(a_v, a_hbm.at[g0 + g])

    _sc_group_loop(groups_per_worker, stage, work)


def _sc_vsum_body(groups_per_worker, v_hbm, idx_hbm, w_hbm, o_hbm, idx_v, w_v, o_v, buf, sems, ssems):
    g0 = _sc_worker_id() * groups_per_worker
    d = o_v.shape[1]
    words_per_pass = SC_LANES // 2
    n_passes = d // (2 * SC_LANES * words_per_pass)

    def stage(g, slot):
        gi = g0 + g
        return [pltpu.make_async_copy(idx_hbm.at[gi], idx_v.at[slot], ssems.at[slot, 0]),
                pltpu.make_async_copy(w_hbm.at[gi], w_v.at[slot], ssems.at[slot, 1])]

    def work(g, slot):
        slot_v = jnp.full((SC_LANES,), slot, jnp.int32)

        def compute(tok, row0, rows):
            tok_v = jnp.full((SC_LANES,), tok, jnp.int32)
            for dg in range(n_passes):
                def body(r, accs):
                    wk = plsc.load_gather(w_v, [slot_v, tok_v, jnp.full((SC_LANES,), row0 + r, jnp.int32)])
                    out = []
                    for cc in range(words_per_pass):
                        lo, hi = _unpack_words(rows[r, pl.ds((dg * words_per_pass + cc) * SC_LANES, SC_LANES)])
                        out += [accs[2 * cc] + wk * lo, accs[2 * cc + 1] + wk * hi]
                    return tuple(out)

                accs = lax.fori_loop(0, SC_ROWS, body, tuple(jnp.zeros((SC_LANES,), F32) for _ in range(SC_LANES)))
                for cc in range(SC_LANES):
                    sl = (tok, pl.ds((dg * SC_LANES + cc) * SC_LANES, SC_LANES))
                    o_v[sl] = o_v[sl] + accs[cc]

        def zero(i, c):
            o_v[i // (d // SC_LANES), pl.ds((i % (d // SC_LANES)) * SC_LANES, SC_LANES)] = jnp.zeros((SC_LANES,), F32)
            return c

        lax.fori_loop(0, SC_GROUP * (d // SC_LANES), zero, 0)
        _sc_gather_loop(v_hbm, idx_v.at[slot], buf, sems, compute)
        pltpu.sync_copy(o_v, o_hbm.at[pl.ds((g0 + g) * SC_GROUP, SC_GROUP)])

    _sc_group_loop(groups_per_worker, stage, work)


def _gate_body(a_ref, g_ref, w_ref):
    a = a_ref[...]
    w_ref[...] = g_ref[...] * (0.5 * a * (1.0 + lax.erf(a * (2.0 ** -0.5))))


def _peer_retrieve(xn, idx, gate, u_tab, v_tab):
    t, d = xn.shape
    hk = idx.shape[1]
    n_groups = t // SC_GROUP
    groups_per_worker = n_groups // (SC_CORES * SC_SUBCORES)
    idx3 = idx.reshape(n_groups, SC_GROUP, hk)
    row_buf = pltpu.VMEM((2, SC_ROWS, d // 2), jnp.int32)
    sems = pltpu.SemaphoreType.DMA((2,))
    stage_sems = pltpu.SemaphoreType.DMA((2, 2))
    cost = pl.CostEstimate(flops=2 * t * hk * d, transcendentals=0,
                           bytes_accessed=2 * t * hk * d + 4 * (t * d + 2 * t * hk))

    a3 = pl.kernel(
        functools.partial(_sc_udot_body, groups_per_worker),
        out_type=jax.ShapeDtypeStruct((n_groups, SC_GROUP, hk), F32),
        mesh=_sc_mesh(),
        scratch_types=[pltpu.VMEM((2, SC_GROUP, hk), jnp.int32), pltpu.VMEM((2, SC_GROUP, d), F32),
                       pltpu.VMEM((SC_GROUP, hk), F32), row_buf, pltpu.VMEM((SC_LANES, SC_LANES), F32), sems,
                       stage_sems],
        compiler_params=pltpu.CompilerParams(needs_layout_passes=False),
        cost_estimate=cost,
        name="peer_udot",
    )(u_tab, idx3, xn)

    def finish():
        g3 = gate.reshape(n_groups, SC_GROUP, hk)
        gb = min(n_groups, 256)
        blk = pl.BlockSpec((gb, SC_GROUP, hk), lambda i: (i, 0, 0))
        w3 = _ordered_call(
            _gate_body, grid=(n_groups // gb,), in_specs=[blk, blk], out_specs=blk,
            out_shape=jax.ShapeDtypeStruct(a3.shape, F32), compiler_params=_params("arbitrary"), name="peer_gate",
        )(a3, g3)
        return pl.kernel(
            functools.partial(_sc_vsum_body, groups_per_worker),
            out_type=jax.ShapeDtypeStruct((t, d), F32),
            mesh=_sc_mesh(),
            scratch_types=[pltpu.VMEM((2, SC_GROUP, hk), jnp.int32), pltpu.VMEM((2, SC_GROUP, hk), F32),
                           pltpu.VMEM((SC_GROUP, d), F32), row_buf, sems, stage_sems],
            compiler_params=pltpu.CompilerParams(needs_layout_passes=False),
            cost_estimate=cost,
            name="peer_vsum",
        )(v_tab, idx3, w3)

    return finish


def _pack_rows_tc(tab):
    n, d = tab.shape
    bits = lax.bitcast_convert_type(tab.astype(BF16), jnp.uint16).astype(jnp.uint32)
    words = (bits[:, d // 2:] << 16) | bits[:, :d // 2]
    return lax.bitcast_convert_type(words, jnp.int32).reshape(n * (d // 2 // LANES), LANES)


def _tc_gather_rows(tab_ref, idx_ref, i, ubuf, n_rows, lines):
    for k in range(n_rows):
        r = idx_ref[i, k]
        ubuf[pl.ds(k, lines, stride=TC_ROW_STRIDE), :] = tab_ref[pl.ds(pl.multiple_of(r * lines, lines), lines), :]


def _tc_halves(ubuf, jb, n_rows):
    wv = ubuf[jb * TC_ROW_STRIDE:jb * TC_ROW_STRIDE + n_rows, :]
    lo = lax.bitcast_convert_type(wv << 16, F32)
    hi = lax.bitcast_convert_type(wv & jnp.int32(-65536), F32)
    return lo, hi


def _tc_udot_body(tab_ref, idx_ref, x_ref, o_ref, ubuf_a, ubuf_b):
    tt, d = x_ref.shape
    n_rows = o_ref.shape[0]
    lines = d // 2 // LANES
    lane = lax.broadcasted_iota(jnp.int32, (n_rows, tt), 1)

    def group(g, out):
        for j in range(SUBLANES):
            i = g * SUBLANES + j
            ubuf = (ubuf_a, ubuf_b)[j % 2]
            _tc_gather_rows(tab_ref, idx_ref, i, ubuf, n_rows, lines)
            xrow = x_ref[pl.ds(i, 1), :]
            acc = jnp.zeros((n_rows, LANES), F32)
            for jb in range(lines):
                lo, hi = _tc_halves(ubuf, jb, n_rows)
                acc = (acc + lo * xrow[:, jb * LANES:(jb + 1) * LANES]
                       + hi * xrow[:, d // 2 + jb * LANES:d // 2 + (jb + 1) * LANES])
            out = jnp.where(lane == i, jnp.sum(acc, axis=1, keepdims=True), out)
        return out

    o_ref[...] = lax.fori_loop(0, tt // SUBLANES, group, jnp.zeros((n_rows, tt), F32))


def _tc_vsum_body(tab_ref, idx_ref, w_ref, o_ref, ubuf_a, ubuf_b):
    tt, d = o_ref.shape
    n_rows = w_ref.shape[0]
    lines = d // 2 // LANES
    lane = lax.broadcasted_iota(jnp.int32, (n_rows, tt), 1)

    def group(g, carry):
        rows = []
        for j in range(SUBLANES):
            i = g * SUBLANES + j
            ubuf = (ubuf_a, ubuf_b)[j % 2]
            _tc_gather_rows(tab_ref, idx_ref, i, ubuf, n_rows, lines)
            col = jnp.sum(jnp.where(lane == i, w_ref[...], 0.0), axis=1, keepdims=True)
            halves = [_tc_halves(ubuf, jb, n_rows) for jb in range(lines)]
            rows.append(jnp.concatenate(
                [jnp.sum(lo * col, axis=0, keepdims=True) for lo, _ in halves]
                + [jnp.sum(hi * col, axis=0, keepdims=True) for _, hi in halves], axis=1))
        o_ref[pl.ds(pl.multiple_of(g * SUBLANES, SUBLANES), SUBLANES), :] = jnp.concatenate(rows, axis=0)
        return carry

    lax.fori_loop(0, tt // SUBLANES, group, 0)


def _tc_table_specs(tab4, hk, tt):
    lines_total = tab4.shape[0]
    return [pl.BlockSpec((lines_total, LANES), lambda i: (0, 0), pipeline_mode=pl.Buffered(1)),
            pl.BlockSpec((tt, hk), lambda i: (i, 0), memory_space=pltpu.SMEM)]


def _tc_udot(tab4, idx, x):
    t, d = x.shape
    hk = idx.shape[1]
    tt = min(TC_RETRIEVE_TILE, t)
    return _ordered_call(
        _tc_udot_body, grid=(t // tt,),
        in_specs=_tc_table_specs(tab4, hk, tt) + [pl.BlockSpec((tt, d), lambda i: (i, 0))],
        out_specs=pl.BlockSpec((hk, tt), lambda i: (0, i)),
        out_shape=jax.ShapeDtypeStruct((hk, t), F32),
        scratch_shapes=[pltpu.VMEM((d // 2 // LANES * TC_ROW_STRIDE, LANES), jnp.int32)] * 2,
        compiler_params=_params("arbitrary"), name="tc_udot",
    )(tab4, idx, x)


def _tc_vsum(tab4, idx, w_t):
    hk, t = w_t.shape
    d = tab4.shape[1] * 2 * (tab4.shape[0] // PEER_NKEYS ** 2)
    tt = min(TC_RETRIEVE_TILE, t)
    return _ordered_call(
        _tc_vsum_body, grid=(t // tt,),
        in_specs=_tc_table_specs(tab4, hk, tt) + [pl.BlockSpec((hk, tt), lambda i: (0, i))],
        out_specs=pl.BlockSpec((tt, d), lambda i: (i, 0)),
        out_shape=jax.ShapeDtypeStruct((t, d), F32),
        scratch_shapes=[pltpu.VMEM((d // 2 // LANES * TC_ROW_STRIDE, LANES), jnp.int32)] * 2,
        compiler_params=_params("arbitrary"), name="tc_vsum",
    )(tab4, idx, w_t)


def _peer_retrieve_tc(xn, idx_t, gate_t, u_tab4, v_tab4):
    hk, t = idx_t.shape
    idx = idx_t.T
    a_t = _tc_udot(u_tab4, idx, xn)

    def finish():
        tt = min(TC_RETRIEVE_TILE, t)
        blk = pl.BlockSpec((hk, tt), lambda i: (0, i))
        w_t = _ordered_call(
            _gate_body, grid=(t // tt,), in_specs=[blk, blk], out_specs=blk,
            out_shape=jax.ShapeDtypeStruct((hk, t), F32), compiler_params=_params("arbitrary"), name="tc_gate",
        )(a_t, gate_t)
        return _tc_vsum(v_tab4, idx, w_t)

    return finish


def _peer_start(x, g, w_q, keys, tables, on_tc):
    bsz, s_len, d = x.shape
    xn, idx_t, gate_t = _peer_route(x.reshape(bsz * s_len, d), g, w_q, keys)
    if on_tc:
        fin = _peer_retrieve_tc(xn, idx_t, gate_t, tables[2], tables[3])
    else:
        fin = _peer_retrieve(xn, idx_t.T, gate_t.T, tables[0], tables[1])
    return lambda: fin().reshape(bsz, s_len, d)


def _kv_body(fox_w, x_ref, po_ref, g_ref, w_ref, bf_ref, k_ref, v_ref, c_ref, ct_ref, carry):
    x = x_ref[0] + po_ref[0]
    tm = x.shape[0]
    hs = _rms(x, g_ref[...]).astype(BF16)
    z = jnp.dot(hs, w_ref[...], preferred_element_type=F32)
    k_ref[0] = z[:, :fox_w].astype(BF16)
    v_ref[0] = z[:, fox_w:2 * fox_w].astype(BF16)
    zf = z[:, 2 * fox_w:] + bf_ref[...]
    logf = jnp.minimum(zf, 0.0) - jnp.log1p(jnp.exp(-jnp.abs(zf)))

    @pl.when(pl.program_id(1) == 0)
    def _():
        carry[...] = jnp.zeros(carry.shape, F32)

    r = lax.broadcasted_iota(jnp.int32, (tm, tm), 0)
    c = lax.broadcasted_iota(jnp.int32, (tm, tm), 1)
    tri = (c <= r).astype(F32)
    cs = jnp.dot(tri, logf, preferred_element_type=F32, precision=lax.Precision.HIGHEST) + carry[...]
    carry[...] = cs[tm - 1:tm, :]
    c_ref[0] = cs
    ct_ref[0] = cs.T[0:ct_ref.shape[1], :]


def _kv_proj(x, po, g, w_kv, b_f):
    bsz, s_len, d = x.shape
    tm = min(TOKEN_TILE, s_len)
    n_heads = b_f.shape[0]
    fox_w = (w_kv.shape[1] - n_heads) // 2
    w = jnp.pad(w_kv, ((0, 0), (0, LANES - n_heads))).astype(BF16)
    bf = jnp.pad(b_f, (0, LANES - n_heads)).reshape(1, LANES)
    hp = -(-n_heads // SUBLANES) * SUBLANES
    tok = lambda w_, dt: (pl.BlockSpec((1, tm, w_), lambda b, i: (b, i, 0)), jax.ShapeDtypeStruct((bsz, s_len, w_), dt))
    (ks, ksh), (vs, vsh), (cs, csh) = tok(fox_w, BF16), tok(fox_w, BF16), tok(LANES, F32)
    return _ordered_call(
        functools.partial(_kv_body, fox_w),
        grid=(bsz, s_len // tm),
        in_specs=[
            pl.BlockSpec((1, tm, d), lambda b, i: (b, i, 0)),
            pl.BlockSpec((1, tm, d), lambda b, i: (b, i, 0)),
            pl.BlockSpec((1, d), lambda b, i: (0, 0)),
            pl.BlockSpec(w.shape, lambda b, i: (0, 0)),
            pl.BlockSpec((1, LANES), lambda b, i: (0, 0)),
        ],
        out_specs=[ks, vs, cs, pl.BlockSpec((1, hp, tm), lambda b, i: (b, 0, i))],
        out_shape=[ksh, vsh, csh, jax.ShapeDtypeStruct((bsz, hp, s_len), F32)],
        scratch_shapes=[pltpu.VMEM((1, LANES), F32)],
        compiler_params=_params("arbitrary", "arbitrary"),
        name="kv_proj",
    )(x, po, g.reshape(1, d), w, bf)


def _mix_b_in_body(fox_w, n_mem_heads, n_mem, x_ref, po_ref, g_ref, win_ref, mk_ref, mv_ref, q_ref, ym_ref):
    x = x_ref[0] + po_ref[0]
    h = _rms(x, g_ref[...]).astype(BF16)
    z = jnp.dot(h, win_ref[...], preferred_element_type=F32)
    q_ref[0] = (z[:, :fox_w] * (HEAD_DIM ** -0.5)).astype(BF16)
    ym_ref[0] = _mem_attend(z[:, fox_w:], mk_ref[0], mv_ref[0], n_mem_heads, n_mem).astype(BF16)


def _mix_b_in(x, po, g, w_in, mkbd, mvbd):
    bsz, s_len, d = x.shape
    tm = min(TOKEN_TILE, s_len)
    mem_w = mvbd.shape[2]
    fox_w = w_in.shape[1] - mem_w
    n_mem_heads = mem_w // HEAD_DIM
    n_mem = mkbd.shape[2] // n_mem_heads
    win = w_in.astype(BF16)
    tok = lambda w_: pl.BlockSpec((1, tm, w_), lambda b, i: (b, i, 0))
    per_b = lambda a: pl.BlockSpec((1,) + a.shape[1:], lambda b, i: (b, 0, 0))
    return _ordered_call(
        functools.partial(_mix_b_in_body, fox_w, n_mem_heads, n_mem),
        grid=(bsz, s_len // tm),
        in_specs=[tok(d), tok(d), pl.BlockSpec((1, d), lambda b, i: (0, 0)),
                  pl.BlockSpec(win.shape, lambda b, i: (0, 0)), per_b(mkbd), per_b(mvbd)],
        out_specs=[tok(fox_w), tok(mem_w)],
        out_shape=[jax.ShapeDtypeStruct((bsz, s_len, fox_w), BF16),
                   jax.ShapeDtypeStruct((bsz, s_len, mem_w), BF16)],
        compiler_params=_params("arbitrary", "arbitrary"),
        name="mix_b_in",
    )(x, po, g.reshape(1, d), win, mkbd, mvbd)


def _fox_body(n_heads, q_ref, k_ref, v_ref, c_ref, ct_ref, o_ref, m_scr, l_scr, cq_scr, acc):
    qi, ki = pl.program_id(1), pl.program_id(2)
    tq, tk = q_ref.shape[1], k_ref.shape[1]

    def wide(a):
        return jnp.concatenate([a] * (tk // LANES), axis=1)

    @pl.when(ki == 0)
    def _():
        m_scr[...] = jnp.full(m_scr.shape, -jnp.inf, F32)
        l_scr[...] = jnp.zeros(l_scr.shape, F32)
        acc[...] = jnp.zeros(acc.shape, F32)
        cq = c_ref[0]
        for h in range(n_heads):
            cq_scr[h] = jnp.broadcast_to(cq[:, h:h + 1], (tq, LANES))

    def step(masked):
        if masked:
            keep = (lax.broadcasted_iota(jnp.int32, (tq, tk), 0)
                    >= lax.broadcasted_iota(jnp.int32, (tq, tk), 1))
        for h in range(n_heads):
            sl = slice(h * HEAD_DIM, (h + 1) * HEAD_DIM)
            s = lax.dot_general(q_ref[0, :, sl], k_ref[0, :, sl], (((1,), (1,)), ((), ())),
                                preferred_element_type=F32)
            s = s + (wide(cq_scr[h]) - ct_ref[0, h:h + 1, :])
            if masked:
                s = jnp.where(keep, s, -jnp.inf)
            m_old = m_scr[h]
            m_new = jnp.maximum(m_old, jnp.max(s, axis=-1, keepdims=True))
            alpha = jnp.exp(m_old - m_new)
            p = jnp.exp(s - wide(m_new))
            l_scr[h] = alpha * l_scr[h] + jnp.sum(p, axis=-1, keepdims=True)
            acc[h] = alpha[:, :HEAD_DIM] * acc[h] + jnp.dot(p.astype(BF16), v_ref[0, :, sl],
                                                           preferred_element_type=F32)
            m_scr[h] = m_new

    @pl.when(ki < qi)
    def _():
        step(False)

    @pl.when(ki == qi)
    def _():
        step(True)
        for h in range(n_heads):
            sl = slice(h * HEAD_DIM, (h + 1) * HEAD_DIM)
            o_ref[0, :, sl] = (acc[h] / l_scr[h][:, :HEAD_DIM]).astype(BF16)


def _fox(q, k, v, c, ct):
    bsz, s_len, w = q.shape
    n_heads = w // HEAD_DIM
    t = min(ATTN_TILE, s_len)
    nb = s_len // t
    return _ordered_call(
        functools.partial(_fox_body, n_heads),
        grid=(bsz, nb, nb),
        in_specs=[
            pl.BlockSpec((1, t, w), lambda b, i, j: (b, i, 0)),
            pl.BlockSpec((1, t, w), lambda b, i, j: (b, jnp.minimum(i, j), 0)),
            pl.BlockSpec((1, t, w), lambda b, i, j: (b, jnp.minimum(i, j), 0)),
            pl.BlockSpec((1, t, c.shape[2]), lambda b, i, j: (b, i, 0)),
            pl.BlockSpec((1, ct.shape[1], t), lambda b, i, j: (b, 0, jnp.minimum(i, j))),
        ],
        out_specs=pl.BlockSpec((1, t, w), lambda b, i, j: (b, i, 0)),
        out_shape=jax.ShapeDtypeStruct((bsz, s_len, w), BF16),
        scratch_shapes=[pltpu.VMEM((n_heads, t, LANES), F32), pltpu.VMEM((n_heads, t, LANES), F32),
                        pltpu.VMEM((n_heads, t, LANES), F32), pltpu.VMEM((n_heads, t, HEAD_DIM), F32)],
        compiler_params=_params("arbitrary", "arbitrary", "arbitrary"),
        name="fox",
    )(q, k, v, c, ct)


def _mix_b_out_body(fox_w, x_ref, po_ref, y_ref, ym_ref, wout_ref, o_ref):
    o_ref[0] = (x_ref[0] + po_ref[0]
                + jnp.dot(y_ref[0], wout_ref[0:fox_w, :], preferred_element_type=F32)
                + jnp.dot(ym_ref[0], wout_ref[fox_w:, :], preferred_element_type=F32))


def _mix_b_out(x, po, y, ym, w_out):
    bsz, s_len, d = x.shape
    tm = min(TOKEN_TILE, s_len)
    fox_w = y.shape[2]
    wout = w_out.astype(BF16)
    tok = lambda w_: pl.BlockSpec((1, tm, w_), lambda b, i: (b, i, 0))
    return _ordered_call(
        functools.partial(_mix_b_out_body, fox_w),
        grid=(bsz, s_len // tm),
        in_specs=[tok(d), tok(d), tok(fox_w), tok(ym.shape[2]), pl.BlockSpec(wout.shape, lambda b, i: (0, 0))],
        out_specs=tok(d),
        out_shape=jax.ShapeDtypeStruct(x.shape, F32),
        compiler_params=_params("arbitrary", "arbitrary"),
        name="mix_b_out",
    )(x, po, y, ym, wout)


def _final_body(x_ref, po_ref, g_ref, o_ref):
    o_ref[0] = _rms(x_ref[0] + po_ref[0], g_ref[...])


def _final_norm(x, po, g):
    bsz, s_len, d = x.shape
    tm = min(TOKEN_TILE, s_len)
    tok = pl.BlockSpec((1, tm, d), lambda b, i: (b, i, 0))
    return _ordered_call(
        _final_body,
        grid=(bsz, s_len // tm),
        in_specs=[tok, tok, pl.BlockSpec((1, d), lambda b, i: (0, 0))],
        out_specs=tok,
        out_shape=jax.ShapeDtypeStruct(x.shape, F32),
        compiler_params=_params("arbitrary", "arbitrary"),
        name="final_norm",
    )(x, po, g.reshape(1, d))


def kernel(x, mem, g_mix, w_in_a, conv_w, w_in_b, w_out, g_mem, w_mem_kv, g_kv, w_kv, b_f, g_ffn,
           peer_wq, peer_keys, peer_u, peer_v, g_final):
    depth = g_mix.shape[0]
    n_a = w_in_a.shape[0]
    n_mem_heads = w_mem_kv.shape[2] // (2 * HEAD_DIM)
    mkbd, mvbd = _block_diag_mem(_memkv(mem, g_mem, w_mem_kv), n_mem_heads)

    bsz = x.shape[0]
    eighths = sum(r for r, _, _ in BATCH_STREAMS)
    plan, unit = (BATCH_STREAMS, bsz // eighths) if bsz % eighths == 0 else (((1, "sc", "sc"),), bsz)
    tables = [(_pack_rows(peer_u[l]), _pack_rows(peer_v[l]),
               _pack_rows_tc(peer_u[l]), _pack_rows_tc(peer_v[l])) for l in range(depth)]
    states, lo = [], 0
    for r, conv_unit, attn_unit in plan:
        rows = r * unit
        states.append(dict(x=x[lo:lo + rows], po=None, lo=lo, hi=lo + rows,
                           units=(conv_unit == "tc", attn_unit == "tc")))
        lo += rows
    _issue.last = None

    def front(st, layer):
        xs, po = st["x"], st["po"]
        mk, mv = mkbd[layer, st["lo"]:st["hi"]], mvbd[layer, st["lo"]:st["hi"]]
        if layer == n_a:
            st["kv"] = _kv_proj(xs, po, g_kv, w_kv, b_f)
        if layer < n_a:
            xs = _mix_a(xs, po, g_mix[layer], w_in_a[layer], conv_w[layer], mk, mv, w_out[layer])
        else:
            q, ym = _mix_b_in(xs, po, g_mix[layer], w_in_b[layer - n_a], mk, mv)
            y = _fox(q, *st["kv"])
            xs = _mix_b_out(xs, po, y, ym, w_out[layer])
        st["x"] = xs
        st["fin"] = _peer_start(xs, g_ffn[layer], peer_wq[layer], peer_keys[layer], tables[layer],
                                st["units"][layer >= n_a])

    def back(st):
        st["po"] = st["fin"]()

    for layer in range(depth):
        waiting = [st for st in states if not st["units"][layer >= n_a]]
        for st in waiting:
            front(st, layer)
        for st in states:
            if st["units"][layer >= n_a]:
                front(st, layer)
                if waiting:
                    back(waiting.pop(0))
                back(st)
                if waiting:
                    back(waiting.pop(0))
        for st in waiting:
            back(st)
    outs = [_final_norm(st["x"], st["po"], g_final) for st in states]
    _issue.last = None
    return jnp.concatenate(outs, axis=0)
```

```python
import functools

import jax
import jax.numpy as jnp
from jax import lax
from jax.experimental import pallas as pl
from jax.experimental.pallas import tpu as pltpu
from jax.experimental.pallas import tpu_sc as plsc

EPS = 1e-6
HEAD_DIM = 64
CONV_K = 3
PEER_HEADS = 8
PEER_NKEYS = 128
PEER_HALF = 128
PEER_TOPK = 16
PEER_CHUNK = 128
LANES = 128
SUBLANES = 8
TOKEN_TILE = 512
ROUTE_TILE = 256
ATTN_TILE = 512
VMEM_LIMIT = 56 * 1024 * 1024
SC_CORES = 2
SC_SUBCORES = 16
SC_LANES = 16
SC_ROWS = 64
SC_GROUP = 16
TC_RETRIEVE_TILE = 128
TC_ROW_STRIDE = 136
COST_MIX_CONV, COST_MIX_ATTN, COST_ROUTE = 0.06, 0.36, 0.29
COST_SC_PASS, COST_TC_UDOT, COST_TC_BACK = 0.70, 0.79, 0.71
BATCH_STREAMS = ((2, "sc", "sc"), (2, "sc", "sc"), (1, "sc", "sc"), (1, "tc", "sc"), (1, "tc", "tc"), (1, "tc", "tc"))

BF16 = jnp.bfloat16
F32 = jnp.float32


def _params(*sem):
    return pltpu.CompilerParams(dimension_semantics=sem, vmem_limit_bytes=VMEM_LIMIT)


class _IssueOrder:
    last = None


_issue = _IssueOrder()


def _ordered_call(body, *, in_specs, **kw):
    n_in = len(in_specs)

    def run(*args):
        prev = _issue.last
        if prev is None:
            outs = pl.pallas_call(body, in_specs=in_specs, **kw)(*args)
        else:
            def with_token(*refs):
                body(*refs[:n_in], *refs[n_in + 1:])

            outs = pl.pallas_call(with_token, in_specs=list(in_specs) + [pl.BlockSpec(memory_space=pl.ANY)],
                                  **kw)(*args, prev)
        _issue.last = jax.tree.leaves(outs)[0]
        return outs

    return run


def _rms(x, g):
    return x * lax.rsqrt(jnp.mean(x * x, axis=-1, keepdims=True) + EPS) * g


def _mem_attend(qm, mkbd, mvbd, n_heads, n_mem):
    lg = jnp.dot(qm.astype(BF16), mkbd, preferred_element_type=F32) * (HEAD_DIM ** -0.5)
    parts = []
    for h in range(n_heads):
        l = lg[:, h * n_mem:(h + 1) * n_mem]
        e = jnp.exp(l - jnp.max(l, axis=-1, keepdims=True))
        parts.append((e / jnp.sum(e, axis=-1, keepdims=True)).astype(BF16))
    return jnp.dot(jnp.concatenate(parts, axis=1), mvbd, preferred_element_type=F32)


def _memkv_body(mem_ref, g_ref, w_ref, o_ref):
    mn = _rms(mem_ref[0], g_ref[0])
    o_ref[0, 0] = jnp.dot(mn.astype(BF16), w_ref[0], preferred_element_type=F32).astype(BF16)


def _memkv(mem, g_mem, w_mem_kv):
    bsz, n_mem, d = mem.shape
    depth, _, wd = w_mem_kv.shape
    return _ordered_call(
        _memkv_body,
        grid=(depth, bsz),
        in_specs=[
            pl.BlockSpec((1, n_mem, d), lambda l, b: (b, 0, 0)),
            pl.BlockSpec((1, 1, d), lambda l, b: (l, 0, 0)),
            pl.BlockSpec((1, d, wd), lambda l, b: (l, 0, 0)),
        ],
        out_specs=pl.BlockSpec((1, 1, n_mem, wd), lambda l, b: (l, b, 0, 0)),
        out_shape=jax.ShapeDtypeStruct((depth, bsz, n_mem, wd), BF16),
        compiler_params=_params("arbitrary", "arbitrary"),
        name="memkv",
    )(mem, g_mem.reshape(depth, 1, d), w_mem_kv.astype(BF16))


def _block_diag_mem(mkv, n_heads):
    depth, bsz, n_mem, _ = mkv.shape
    r = mkv.reshape(depth, bsz, n_mem, 2, n_heads, HEAD_DIM)
    eye = jnp.eye(n_heads, dtype=mkv.dtype)
    mk = jnp.einsum("lbmhd,hg->lbhdgm", r[:, :, :, 0], eye)
    mv = jnp.einsum("lbmhd,hg->lbhmgd", r[:, :, :, 1], eye)
    hw = n_heads * HEAD_DIM
    return mk.reshape(depth, bsz, hw, n_heads * n_mem), mv.reshape(depth, bsz, n_heads * n_mem, hw)


def _mix_a_body(has_po, conv_ch, n_mem_heads, n_mem, *refs):
    if has_po:
        x_ref, po_ref, g_ref, win_ref, cw_ref, mk_ref, mv_ref, wout_ref, o_ref, cbuf = refs
        x = x_ref[0] + po_ref[0]
    else:
        x_ref, g_ref, win_ref, cw_ref, mk_ref, mv_ref, wout_ref, o_ref, cbuf = refs
        x = x_ref[0]
    tm = x.shape[0]
    h = _rms(x, g_ref[...]).astype(BF16)
    z = jnp.dot(h, win_ref[...], preferred_element_type=F32)
    u = z[:, :conv_ch]
    c_gate = z[:, conv_ch:2 * conv_ch]
    b_gate = z[:, 2 * conv_ch:3 * conv_ch]
    qm = z[:, 3 * conv_ch:]
    cu = c_gate * u

    @pl.when(pl.program_id(1) == 0)
    def _():
        cbuf[0:SUBLANES, :] = jnp.zeros((SUBLANES, conv_ch), F32)

    cbuf[SUBLANES:SUBLANES + tm, :] = cu
    cu1 = cbuf[SUBLANES - 1:SUBLANES - 1 + tm, :]
    cu2 = cbuf[SUBLANES - 2:SUBLANES - 2 + tm, :]
    cw = cw_ref[...]
    y = b_gate * (cw[0:1] * cu2 + cw[1:2] * cu1 + cw[2:3] * cu)
    cbuf[0:SUBLANES, :] = cbuf[tm:tm + SUBLANES, :]

    ymem = _mem_attend(qm, mk_ref[0], mv_ref[0], n_mem_heads, n_mem)
    o_ref[0] = (x
                + jnp.dot(y.astype(BF16), wout_ref[0:conv_ch, :], preferred_element_type=F32)
                + jnp.dot(ymem.astype(BF16), wout_ref[conv_ch:, :], preferred_element_type=F32))


def _mix_a(x, po, g, w_in, conv_w, mkbd, mvbd, w_out):
    bsz, s_len, d = x.shape
    tm = min(TOKEN_TILE, s_len)
    conv_ch = conv_w.shape[1]
    mem_w = w_in.shape[1] - 3 * conv_ch
    n_mem_heads = mem_w // HEAD_DIM
    n_mem = mkbd.shape[2] // n_mem_heads
    tok = pl.BlockSpec((1, tm, d), lambda b, i: (b, i, 0))
    full = lambda a: pl.BlockSpec(a.shape, lambda b, i: (0,) * a.ndim)
    per_b = lambda a: pl.BlockSpec((1,) + a.shape[1:], lambda b, i: (b, 0, 0))
    g2 = g.reshape(1, d)
    win = w_in.astype(BF16)
    wout = w_out.astype(BF16)
    args = [x] + ([po] if po is not None else []) + [g2, win, conv_w, mkbd, mvbd, wout]
    specs = [tok] + ([tok] if po is not None else []) + [full(g2), full(win), full(conv_w), per_b(mkbd), per_b(mvbd), full(wout)]
    return _ordered_call(
        functools.partial(_mix_a_body, po is not None, conv_ch, n_mem_heads, n_mem),
        grid=(bsz, s_len // tm),
        in_specs=specs,
        out_specs=tok,
        out_shape=jax.ShapeDtypeStruct(x.shape, F32),
        scratch_shapes=[pltpu.VMEM((tm + SUBLANES, conv_ch), F32)],
        compiler_params=_params("arbitrary", "arbitrary"),
        name="mix_a",
    )(*args)


def _topk_rows(s, k, payload=None):
    n, cols = s.shape
    n_slabs = n // SUBLANES
    sub = lax.broadcasted_iota(jnp.int32, (SUBLANES, cols), 0)
    vals = [s[i * SUBLANES:(i + 1) * SUBLANES] for i in range(n_slabs)]
    rows = [sub + i * SUBLANES for i in range(n_slabs)]
    pays = None if payload is None else [payload[i * SUBLANES:(i + 1) * SUBLANES] for i in range(n_slabs)]
    out_v, out_i = [], []
    for _ in range(k):
        items = [(v, r) + (() if pays is None else (p,)) for v, r, p in zip(vals, rows, pays or rows)]
        while len(items) > 1:
            nxt = []
            for a, b in zip(items[0::2], items[1::2]):
                keep_a = a[0] >= b[0]
                nxt.append(tuple(jnp.where(keep_a, x, y) for x, y in zip(a, b)))
            if len(items) % 2:
                nxt.append(items[-1])
            items = nxt
        best = items[0]
        m = jnp.max(best[0], axis=0, keepdims=True)
        pos = jnp.min(jnp.where(best[0] == m, best[1], n), axis=0, keepdims=True)
        out_v.append(m)
        if pays is None:
            out_i.append(pos)
        else:
            out_i.append(jnp.sum(jnp.where(best[1] == pos, best[2], 0), axis=0, keepdims=True))
        vals = [jnp.where(r == pos, -jnp.inf, v) for v, r in zip(vals, rows)]
    return jnp.concatenate(out_v, axis=0), jnp.concatenate(out_i, axis=0)


def _route_body(x_ref, g_ref, wq_ref, keys_ref, xn_ref, idx_ref, gate_ref, xnb):
    @pl.when(pl.program_id(1) == 0)
    def _():
        xn = _rms(x_ref[...], g_ref[...])
        xn_ref[...] = xn
        xnb[...] = xn.astype(BF16)

    q = jnp.dot(xnb[...], wq_ref[...], preferred_element_type=F32)
    tops = []
    for p in range(2):
        qp = q[:, p * PEER_HALF:(p + 1) * PEER_HALF].astype(BF16)
        st = lax.dot_general(keys_ref[0, p], qp, (((1,), (1,)), ((), ())),
                             preferred_element_type=F32)
        tops.append(_topk_rows(st, PEER_TOPK))
    (v1, i1), (v2, i2) = tops
    cand, cidx, i = [], [], 0
    while i < PEER_TOPK:
        n_j = PEER_TOPK // (i + 1)
        if n_j > 1:
            n_j = min(PEER_TOPK, -(-n_j // SUBLANES) * SUBLANES)
            cand.append(v1[i:i + 1] + v2[0:n_j])
            cidx.append(i1[i:i + 1] * PEER_NKEYS + i2[0:n_j])
            i += 1
        else:
            cand.append(v1[i:] + v2[0:1])
            cidx.append(i1[i:] * PEER_NKEYS + i2[0:1])
            i = PEER_TOPK
    top_s, idx = _topk_rows(jnp.concatenate(cand, axis=0), PEER_TOPK, payload=jnp.concatenate(cidx, axis=0))
    e = jnp.exp(top_s - top_s[0:1])
    idx_ref[...] = idx
    gate_ref[...] = e / jnp.sum(e, axis=0, keepdims=True)


def _peer_route(x2, g, w_q, keys):
    t, d = x2.shape
    tm = min(ROUTE_TILE, t)
    dk = 2 * PEER_HALF
    return _ordered_call(
        _route_body,
        grid=(t // tm, PEER_HEADS),
        in_specs=[
            pl.BlockSpec((tm, d), lambda i, h: (i, 0)),
            pl.BlockSpec((1, d), lambda i, h: (0, 0)),
            pl.BlockSpec((d, dk), lambda i, h: (0, h)),
            pl.BlockSpec((1, 2, PEER_NKEYS, PEER_HALF), lambda i, h: (h, 0, 0, 0)),
        ],
        out_specs=[
            pl.BlockSpec((tm, d), lambda i, h: (i, 0)),
            pl.BlockSpec((PEER_TOPK, tm), lambda i, h: (h, i)),
            pl.BlockSpec((PEER_TOPK, tm), lambda i, h: (h, i)),
        ],
        out_shape=[
            jax.ShapeDtypeStruct((t, d), F32),
            jax.ShapeDtypeStruct((PEER_HEADS * PEER_TOPK, t), jnp.int32),
            jax.ShapeDtypeStruct((PEER_HEADS * PEER_TOPK, t), F32),
        ],
        scratch_shapes=[pltpu.VMEM((tm, d), BF16)],
        compiler_params=_params("arbitrary", "arbitrary"),
        name="peer_route",
    )(x2, g.reshape(1, d), w_q.astype(BF16), keys.astype(BF16))


def _sc_mesh():
    return plsc.VectorSubcoreMesh(core_axis_name="c", subcore_axis_name="s",
                                  num_cores=SC_CORES, num_subcores=SC_SUBCORES)


def _sc_worker_id():
    return lax.axis_index("s") * SC_CORES + lax.axis_index("c")


def _sc_gather_loop(tab_hbm, idx_v, buf, sems, compute):
    per_tok = idx_v.shape[1] // SC_ROWS
    n_batches = SC_GROUP * per_tok

    def gather(j, slot):
        tok = j // per_tok
        row0 = pl.multiple_of((j % per_tok) * SC_ROWS, SC_ROWS)
        return tok, row0, pltpu.make_async_copy(
            tab_hbm.at[idx_v.at[tok, pl.ds(row0, SC_ROWS)]], buf.at[slot], sems.at[slot])

    nbuf = buf.shape[0]
    for b in range(nbuf - 1):
        gather(b, b)[2].start()

    def ring(jj, carry):
        for b in range(nbuf):
            j = jj * nbuf + b
            ahead = j + nbuf - 1

            @pl.when(ahead < n_batches)
            def _():
                gather(ahead, (b + nbuf - 1) % nbuf)[2].start()

            tok, row0, cp = gather(j, b)
            cp.wait()
            compute(tok, row0, buf.at[b])
        return carry

    lax.fori_loop(0, n_batches // nbuf, ring, 0)


def _pack_pairs(lo, hi):
    lo_bits = lax.bitcast_convert_type(lo.astype(BF16), jnp.uint16).astype(jnp.uint32)
    h = lax.bitcast_convert_type(hi, jnp.uint32)
    mag = (h & jnp.uint32(0x7FFFFFFF)) + jnp.uint32(0x8000)
    top = jnp.where(mag >= lo_bits, (mag - lo_bits) >> 16, jnp.uint32(0))
    return lax.bitcast_convert_type((h & jnp.uint32(0x80000000)) | (top << 16) | lo_bits, jnp.int32)


def _unpack_words(w):
    return lax.bitcast_convert_type(w << 16, F32), lax.bitcast_convert_type(w, F32)


def _pack_rows(tab):
    n, d = tab.shape
    t4 = tab.reshape(n, d // (2 * SC_LANES), 2, SC_LANES)
    return _pack_pairs(t4[:, :, 0, :], t4[:, :, 1, :]).reshape(n, d // 2)


def _sc_group_loop(n_groups, stage, work):
    for cp in stage(0, 0):
        cp.start()

    def group(g, carry):
        slot = g % 2
        for cp in stage(g, slot):
            cp.wait()

        @pl.when(g + 1 < n_groups)
        def _():
            for cp in stage(g + 1, 1 - slot):
                cp.start()

        work(g, slot)
        return carry

    lax.fori_loop(0, n_groups, group, 0)


def _sc_udot_body(groups_per_worker, u_hbm, idx_hbm, x_hbm, a_hbm, idx_v, x_v, a_v, buf, tr, sems, ssems):
    g0 = _sc_worker_id() * groups_per_worker
    lanes = lax.iota(jnp.int32, SC_LANES)
    chunks = x_v.shape[2] // (2 * SC_LANES)

    def stage(g, slot):
        gi = g0 + g
        return [pltpu.make_async_copy(idx_hbm.at[gi], idx_v.at[slot], ssems.at[slot, 0]),
                pltpu.make_async_copy(x_hbm.at[pl.ds(gi * SC_GROUP, SC_GROUP)], x_v.at[slot], ssems.at[slot, 1])]

    def work(g, slot):
        def compute(tok, row0, rows):
            for eb in range(SC_ROWS // SC_LANES):
                def body(c, accs):
                    x_lo = x_v[slot, tok, pl.ds(c * 2 * SC_LANES, SC_LANES)]
                    x_hi = x_v[slot, tok, pl.ds(c * 2 * SC_LANES + SC_LANES, SC_LANES)]
                    out = []
                    for e in range(SC_LANES):
                        lo, hi = _unpack_words(rows[eb * SC_LANES + e, pl.ds(c * SC_LANES, SC_LANES)])
                        out.append(accs[e] + lo * x_lo + hi * x_hi)
                    return tuple(out)

                accs = lax.fori_loop(0, chunks, body, tuple(jnp.zeros((SC_LANES,), F32) for _ in range(SC_LANES)))
                for e in range(SC_LANES):
                    tr[e, :] = accs[e]
                tot = jnp.zeros((SC_LANES,), F32)
                for c in range(SC_LANES):
                    tot = tot + plsc.load_gather(tr, [lanes, jnp.full((SC_LANES,), c, jnp.int32)])
                a_v[tok, pl.ds(row0 + eb * SC_LANES, SC_LANES)] = tot

        _sc_gather_loop(u_hbm, idx_v.at[slot], buf, sems, compute)
        pltpu.sync_copy(a_v, a_hbm.at[g0 + g])

    _sc_group_loop(groups_per_worker, stage, work)


def _sc_vsum_body(groups_per_worker, v_hbm, idx_hbm, w_hbm, o_hbm, idx_v, w_v, o_v, buf, sems, ssems):
    g0 = _sc_worker_id() * groups_per_worker
    d = o_v.shape[1]
    words_per_pass = SC_LANES // 2
    n_passes = d // (2 * SC_LANES * words_per_pass)

    def stage(g, slot):
        gi = g0 + g
        return [pltpu.make_async_copy(idx_hbm.at[gi], idx_v.at[slot], ssems.at[slot, 0]),
                pltpu.make_async_copy(w_hbm.at[gi], w_v.at[slot], ssems.at[slot, 1])]

    def work(g, slot):
        slot_v = jnp.full((SC_LANES,), slot, jnp.int32)

        def compute(tok, row0, rows):
            tok_v = jnp.full((SC_LANES,), tok, jnp.int32)
            for dg in range(n_passes):
                def body(r, accs):
                    wk = plsc.load_gather(w_v, [slot_v, tok_v, jnp.full((SC_LANES,), row0 + r, jnp.int32)])
                    out = []
                    for cc in range(words_per_pass):
                        lo, hi = _unpack_words(rows[r, pl.ds((dg * words_per_pass + cc) * SC_LANES, SC_LANES)])
                        out += [accs[2 * cc] + wk * lo, accs[2 * cc + 1] + wk * hi]
                    return tuple(out)

                accs = lax.fori_loop(0, SC_ROWS, body, tuple(jnp.zeros((SC_LANES,), F32) for _ in range(SC_LANES)))
                for cc in range(SC_LANES):
                    sl = (tok, pl.ds((dg * SC_LANES + cc) * SC_LANES, SC_LANES))
                    o_v[sl] = o_v[sl] + accs[cc]

        def zero(i, c):
            o_v[i // (d // SC_LANES), pl.ds((i % (d // SC_LANES)) * SC_LANES, SC_LANES)] = jnp.zeros((SC_LANES,), F32)
            return c

        lax.fori_loop(0, SC_GROUP * (d // SC_LANES), zero, 0)
        _sc_gather_loop(v_hbm, idx_v.at[slot], buf, sems, compute)
        pltpu.sync_copy(o_v, o_hbm.at[pl.ds((g0 + g) * SC_GROUP, SC_GROUP)])

    _sc_group_loop(groups_per_worker, stage, work)


def _gate_body(a_ref, g_ref, w_ref):
    a = a_ref[...]
    w_ref[...] = g_ref[...] * (0.5 * a * (1.0 + lax.erf(a * (2.0 ** -0.5))))


def _peer_retrieve(xn, idx, gate, u_tab, v_tab):
    t, d = xn.shape
    hk = idx.shape[1]
    n_groups = t // SC_GROUP
    groups_per_worker = n_groups // (SC_CORES * SC_SUBCORES)
    idx3 = idx.reshape(n_groups, SC_GROUP, hk)
    row_buf = pltpu.VMEM((2, SC_ROWS, d // 2), jnp.int32)
    sems = pltpu.SemaphoreType.DMA((2,))
    stage_sems = pltpu.SemaphoreType.DMA((2, 2))
    cost = pl.CostEstimate(flops=2 * t * hk * d, transcendentals=0,
                           bytes_accessed=2 * t * hk * d + 4 * (t * d + 2 * t * hk))

    a3 = pl.kernel(
        functools.partial(_sc_udot_body, groups_per_worker),
        out_type=jax.ShapeDtypeStruct((n_groups, SC_GROUP, hk), F32),
        mesh=_sc_mesh(),
        scratch_types=[pltpu.VMEM((2, SC_GROUP, hk), jnp.int32), pltpu.VMEM((2, SC_GROUP, d), F32),
                       pltpu.VMEM((SC_GROUP, hk), F32), row_buf, pltpu.VMEM((SC_LANES, SC_LANES), F32), sems,
                       stage_sems],
        compiler_params=pltpu.CompilerParams(needs_layout_passes=False),
        cost_estimate=cost,
        name="peer_udot",
    )(u_tab, idx3, xn)

    def finish():
        g3 = gate.reshape(n_groups, SC_GROUP, hk)
        gb = min(n_groups, 256)
        blk = pl.BlockSpec((gb, SC_GROUP, hk), lambda i: (i, 0, 0))
        w3 = _ordered_call(
            _gate_body, grid=(n_groups // gb,), in_specs=[blk, blk], out_specs=blk,
            out_shape=jax.ShapeDtypeStruct(a3.shape, F32), compiler_params=_params("arbitrary"), name="peer_gate",
        )(a3, g3)
        return pl.kernel(
            functools.partial(_sc_vsum_body, groups_per_worker),
            out_type=jax.ShapeDtypeStruct((t, d), F32),
            mesh=_sc_mesh(),
            scratch_types=[pltpu.VMEM((2, SC_GROUP, hk), jnp.int32), pltpu.VMEM((2, SC_GROUP, hk), F32),
                           pltpu.VMEM((SC_GROUP, d), F32), row_buf, sems, stage_sems],
            compiler_params=pltpu.CompilerParams(needs_layout_passes=False),
            cost_estimate=cost,
            name="peer_vsum",
        )(v_tab, idx3, w3)

    return finish


def _pack_rows_tc(tab):
    n, d = tab.shape
    return _pack_pairs(tab[:, :d // 2], tab[:, d // 2:]).reshape(n * (d // 2 // LANES), LANES)


def _tc_gather_rows(tab_ref, idx_ref, i, ubuf, n_rows, lines):
    for k in range(n_rows):
        r = idx_ref[i, k]
        ubuf[pl.ds(k, lines, stride=TC_ROW_STRIDE), :] = tab_ref[pl.ds(pl.multiple_of(r * lines, lines), lines), :]


def _tc_halves(ubuf, jb, n_rows):
    return _unpack_words(ubuf[jb * TC_ROW_STRIDE:jb * TC_ROW_STRIDE + n_rows, :])


def _tc_udot_body(tab_ref, idx_ref, x_ref, o_ref, ubuf_a, ubuf_b):
    tt, d = x_ref.shape
    n_rows = o_ref.shape[0]
    lines = d // 2 // LANES
    lane = lax.broadcasted_iota(jnp.int32, (n_rows, tt), 1)

    def group(g, out):
        for j in range(SUBLANES):
            i = g * SUBLANES + j
            ubuf = (ubuf_a, ubuf_b)[j % 2]
            _tc_gather_rows(tab_ref, idx_ref, i, ubuf, n_rows, lines)
            xrow = x_ref[pl.ds(i, 1), :]
            acc = jnp.zeros((n_rows, LANES), F32)
            for jb in range(lines):
                lo, hi = _tc_halves(ubuf, jb, n_rows)
                acc = (acc + lo * xrow[:, jb * LANES:(jb + 1) * LANES]
                       + hi * xrow[:, d // 2 + jb * LANES:d // 2 + (jb + 1) * LANES])
            out = jnp.where(lane == i, jnp.sum(acc, axis=1, keepdims=True), out)
        return out

    o_ref[...] = lax.fori_loop(0, tt // SUBLANES, group, jnp.zeros((n_rows, tt), F32))


def _tc_vsum_body(tab_ref, idx_ref, w_ref, o_ref, ubuf_a, ubuf_b):
    tt, d = o_ref.shape
    n_rows = w_ref.shape[0]
    lines = d // 2 // LANES
    lane = lax.broadcasted_iota(jnp.int32, (n_rows, tt), 1)

    def group(g, carry):
        rows = []
        for j in range(SUBLANES):
            i = g * SUBLANES + j
            ubuf = (ubuf_a, ubuf_b)[j % 2]
            _tc_gather_rows(tab_ref, idx_ref, i, ubuf, n_rows, lines)
            col = jnp.sum(jnp.where(lane == i, w_ref[...], 0.0), axis=1, keepdims=True)
            halves = [_tc_halves(ubuf, jb, n_rows) for jb in range(lines)]
            rows.append(jnp.concatenate(
                [jnp.sum(lo * col, axis=0, keepdims=True) for lo, _ in halves]
                + [jnp.sum(hi * col, axis=0, keepdims=True) for _, hi in halves], axis=1))
        o_ref[pl.ds(pl.multiple_of(g * SUBLANES, SUBLANES), SUBLANES), :] = jnp.concatenate(rows, axis=0)
        return carry

    lax.fori_loop(0, tt // SUBLANES, group, 0)


def _tc_table_specs(tab4, hk, tt):
    lines_total = tab4.shape[0]
    return [pl.BlockSpec((lines_total, LANES), lambda i: (0, 0), pipeline_mode=pl.Buffered(1)),
            pl.BlockSpec((tt, hk), lambda i: (i, 0), memory_space=pltpu.SMEM)]


def _tc_udot(tab4, idx, x):
    t, d = x.shape
    hk = idx.shape[1]
    tt = min(TC_RETRIEVE_TILE, t)
    return _ordered_call(
        _tc_udot_body, grid=(t // tt,),
        in_specs=_tc_table_specs(tab4, hk, tt) + [pl.BlockSpec((tt, d), lambda i: (i, 0))],
        out_specs=pl.BlockSpec((hk, tt), lambda i: (0, i)),
        out_shape=jax.ShapeDtypeStruct((hk, t), F32),
        scratch_shapes=[pltpu.VMEM((d // 2 // LANES * TC_ROW_STRIDE, LANES), jnp.int32)] * 2,
        compiler_params=_params("arbitrary"), name="tc_udot",
    )(tab4, idx, x)


def _tc_vsum(tab4, idx, w_t):
    hk, t = w_t.shape
    d = tab4.shape[1] * 2 * (tab4.shape[0] // PEER_NKEYS ** 2)
    tt = min(TC_RETRIEVE_TILE, t)
    return _ordered_call(
        _tc_vsum_body, grid=(t // tt,),
        in_specs=_tc_table_specs(tab4, hk, tt) + [pl.BlockSpec((hk, tt), lambda i: (0, i))],
        out_specs=pl.BlockSpec((tt, d), lambda i: (i, 0)),
        out_shape=jax.ShapeDtypeStruct((t, d), F32),
        scratch_shapes=[pltpu.VMEM((d // 2 // LANES * TC_ROW_STRIDE, LANES), jnp.int32)] * 2,
        compiler_params=_params("arbitrary"), name="tc_vsum",
    )(tab4, idx, w_t)


def _peer_retrieve_tc(xn, idx_t, gate_t, u_tab4, v_tab4):
    hk, t = idx_t.shape
    idx = idx_t.T
    a_t = _tc_udot(u_tab4, idx, xn)

    def finish():
        tt = min(TC_RETRIEVE_TILE, t)
        blk = pl.BlockSpec((hk, tt), lambda i: (0, i))
        w_t = _ordered_call(
            _gate_body, grid=(t // tt,), in_specs=[blk, blk], out_specs=blk,
            out_shape=jax.ShapeDtypeStruct((hk, t), F32), compiler_params=_params("arbitrary"), name="tc_gate",
        )(a_t, gate_t)
        return _tc_vsum(v_tab4, idx, w_t)

    return finish


def _peer_start(x, g, w_q, keys, tables, on_tc):
    bsz, s_len, d = x.shape
    xn, idx_t, gate_t = _peer_route(x.reshape(bsz * s_len, d), g, w_q, keys)
    if on_tc:
        fin = _peer_retrieve_tc(xn, idx_t, gate_t, tables[2], tables[3])
    else:
        fin = _peer_retrieve(xn, idx_t.T, gate_t.T, tables[0], tables[1])
    return lambda: fin().reshape(bsz, s_len, d)


def _kv_body(fox_w, x_ref, po_ref, g_ref, w_ref, bf_ref, k_ref, v_ref, c_ref, ct_ref, carry):
    x = x_ref[0] + po_ref[0]
    tm = x.shape[0]
    hs = _rms(x, g_ref[...]).astype(BF16)
    z = jnp.dot(hs, w_ref[...], preferred_element_type=F32)
    k_ref[0] = z[:, :fox_w].astype(BF16)
    v_ref[0] = z[:, fox_w:2 * fox_w].astype(BF16)
    zf = z[:, 2 * fox_w:] + bf_ref[...]
    logf = jnp.minimum(zf, 0.0) - jnp.log1p(jnp.exp(-jnp.abs(zf)))

    @pl.when(pl.program_id(1) == 0)
    def _():
        carry[...] = jnp.zeros(carry.shape, F32)

    r = lax.broadcasted_iota(jnp.int32, (tm, tm), 0)
    c = lax.broadcasted_iota(jnp.int32, (tm, tm), 1)
    tri = (c <= r).astype(F32)
    cs = jnp.dot(tri, logf, preferred_element_type=F32, precision=lax.Precision.HIGHEST) + carry[...]
    carry[...] = cs[tm - 1:tm, :]
    c_ref[0] = cs
    ct_ref[0] = cs.T[0:ct_ref.shape[1], :]


def _kv_proj(x, po, g, w_kv, b_f):
    bsz, s_len, d = x.shape
    tm = min(TOKEN_TILE, s_len)
    n_heads = b_f.shape[0]
    fox_w = (w_kv.shape[1] - n_heads) // 2
    w = jnp.pad(w_kv, ((0, 0), (0, LANES - n_heads))).astype(BF16)
    bf = jnp.pad(b_f, (0, LANES - n_heads)).reshape(1, LANES)
    hp = -(-n_heads // SUBLANES) * SUBLANES
    tok = lambda w_, dt: (pl.BlockSpec((1, tm, w_), lambda b, i: (b, i, 0)), jax.ShapeDtypeStruct((bsz, s_len, w_), dt))
    (ks, ksh), (vs, vsh), (cs, csh) = tok(fox_w, BF16), tok(fox_w, BF16), tok(LANES, F32)
    return _ordered_call(
        functools.partial(_kv_body, fox_w),
        grid=(bsz, s_len // tm),
        in_specs=[
            pl.BlockSpec((1, tm, d), lambda b, i: (b, i, 0)),
            pl.BlockSpec((1, tm, d), lambda b, i: (b, i, 0)),
            pl.BlockSpec((1, d), lambda b, i: (0, 0)),
            pl.BlockSpec(w.shape, lambda b, i: (0, 0)),
            pl.BlockSpec((1, LANES), lambda b, i: (0, 0)),
        ],
        out_specs=[ks, vs, cs, pl.BlockSpec((1, hp, tm), lambda b, i: (b, 0, i))],
        out_shape=[ksh, vsh, csh, jax.ShapeDtypeStruct((bsz, hp, s_len), F32)],
        scratch_shapes=[pltpu.VMEM((1, LANES), F32)],
        compiler_params=_params("arbitrary", "arbitrary"),
        name="kv_proj",
    )(x, po, g.reshape(1, d), w, bf)


def _mix_b_in_body(fox_w, n_mem_heads, n_mem, x_ref, po_ref, g_ref, win_ref, mk_ref, mv_ref, q_ref, ym_ref):
    x = x_ref[0] + po_ref[0]
    h = _rms(x, g_ref[...]).astype(BF16)
    z = jnp.dot(h, win_ref[...], preferred_element_type=F32)
    q_ref[0] = (z[:, :fox_w] * (HEAD_DIM ** -0.5)).astype(BF16)
    ym_ref[0] = _mem_attend(z[:, fox_w:], mk_ref[0], mv_ref[0], n_mem_heads, n_mem).astype(BF16)


def _mix_b_in(x, po, g, w_in, mkbd, mvbd):
    bsz, s_len, d = x.shape
    tm = min(TOKEN_TILE, s_len)
    mem_w = mvbd.shape[2]
    fox_w = w_in.shape[1] - mem_w
    n_mem_heads = mem_w // HEAD_DIM
    n_mem = mkbd.shape[2] // n_mem_heads
    win = w_in.astype(BF16)
    tok = lambda w_: pl.BlockSpec((1, tm, w_), lambda b, i: (b, i, 0))
    per_b = lambda a: pl.BlockSpec((1,) + a.shape[1:], lambda b, i: (b, 0, 0))
    return _ordered_call(
        functools.partial(_mix_b_in_body, fox_w, n_mem_heads, n_mem),
        grid=(bsz, s_len // tm),
        in_specs=[tok(d), tok(d), pl.BlockSpec((1, d), lambda b, i: (0, 0)),
                  pl.BlockSpec(win.shape, lambda b, i: (0, 0)), per_b(mkbd), per_b(mvbd)],
        out_specs=[tok(fox_w), tok(mem_w)],
        out_shape=[jax.ShapeDtypeStruct((bsz, s_len, fox_w), BF16),
                   jax.ShapeDtypeStruct((bsz, s_len, mem_w), BF16)],
        compiler_params=_params("arbitrary", "arbitrary"),
        name="mix_b_in",
    )(x, po, g.reshape(1, d), win, mkbd, mvbd)


def _fox_body(n_heads, q_ref, k_ref, v_ref, c_ref, ct_ref, o_ref, m_scr, l_scr, cq_scr, acc):
    qi, ki = pl.program_id(1), pl.program_id(2)
    tq, tk = q_ref.shape[1], k_ref.shape[1]

    def wide(a):
        return jnp.concatenate([a] * (tk // LANES), axis=1)

    @pl.when(ki == 0)
    def _():
        m_scr[...] = jnp.full(m_scr.shape, -jnp.inf, F32)
        l_scr[...] = jnp.zeros(l_scr.shape, F32)
        acc[...] = jnp.zeros(acc.shape, F32)
        cq = c_ref[0]
        for h in range(n_heads):
            cq_scr[h] = jnp.broadcast_to(cq[:, h:h + 1], (tq, LANES))

    def step(masked):
        if masked:
            keep = (lax.broadcasted_iota(jnp.int32, (tq, tk), 0)
                    >= lax.broadcasted_iota(jnp.int32, (tq, tk), 1))
        for h in range(n_heads):
            sl = slice(h * HEAD_DIM, (h + 1) * HEAD_DIM)
            s = lax.dot_general(q_ref[0, :, sl], k_ref[0, :, sl], (((1,), (1,)), ((), ())),
                                preferred_element_type=F32)
            s = s + (wide(cq_scr[h]) - ct_ref[0, h:h + 1, :])
            if masked:
                s = jnp.where(keep, s, -jnp.inf)
            m_old = m_scr[h]
            m_new = jnp.maximum(m_old, jnp.max(s, axis=-1, keepdims=True))
            alpha = jnp.exp(m_old - m_new)
            p = jnp.exp(s - wide(m_new))
            l_scr[h] = alpha * l_scr[h] + jnp.sum(p, axis=-1, keepdims=True)
            acc[h] = alpha[:, :HEAD_DIM] * acc[h] + jnp.dot(p.astype(BF16), v_ref[0, :, sl],
                                                           preferred_element_type=F32)
            m_scr[h] = m_new

    @pl.when(ki < qi)
    def _():
        step(False)

    @pl.when(ki == qi)
    def _():
        step(True)
        for h in range(n_heads):
            sl = slice(h * HEAD_DIM, (h + 1) * HEAD_DIM)
            o_ref[0, :, sl] = (acc[h] / l_scr[h][:, :HEAD_DIM]).astype(BF16)


def _fox(q, k, v, c, ct):
    bsz, s_len, w = q.shape
    n_heads = w // HEAD_DIM
    t = min(ATTN_TILE, s_len)
    nb = s_len // t
    return _ordered_call(
        functools.partial(_fox_body, n_heads),
        grid=(bsz, nb, nb),
        in_specs=[
            pl.BlockSpec((1, t, w), lambda b, i, j: (b, i, 0)),
            pl.BlockSpec((1, t, w), lambda b, i, j: (b, jnp.minimum(i, j), 0)),
            pl.BlockSpec((1, t, w), lambda b, i, j: (b, jnp.minimum(i, j), 0)),
            pl.BlockSpec((1, t, c.shape[2]), lambda b, i, j: (b, i, 0)),
            pl.BlockSpec((1, ct.shape[1], t), lambda b, i, j: (b, 0, jnp.minimum(i, j))),
        ],
        out_specs=pl.BlockSpec((1, t, w), lambda b, i, j: (b, i, 0)),
        out_shape=jax.ShapeDtypeStruct((bsz, s_len, w), BF16),
        scratch_shapes=[pltpu.VMEM((n_heads, t, LANES), F32), pltpu.VMEM((n_heads, t, LANES), F32),
                        pltpu.VMEM((n_heads, t, LANES), F32), pltpu.VMEM((n_heads, t, HEAD_DIM), F32)],
        compiler_params=_params("arbitrary", "arbitrary", "arbitrary"),
        name="fox",
    )(q, k, v, c, ct)


def _mix_b_out_body(fox_w, x_ref, po_ref, y_ref, ym_ref, wout_ref, o_ref):
    o_ref[0] = (x_ref[0] + po_ref[0]
                + jnp.dot(y_ref[0], wout_ref[0:fox_w, :], preferred_element_type=F32)
                + jnp.dot(ym_ref[0], wout_ref[fox_w:, :], preferred_element_type=F32))


def _mix_b_out(x, po, y, ym, w_out):
    bsz, s_len, d = x.shape
    tm = min(TOKEN_TILE, s_len)
    fox_w = y.shape[2]
    wout = w_out.astype(BF16)
    tok = lambda w_: pl.BlockSpec((1, tm, w_), lambda b, i: (b, i, 0))
    return _ordered_call(
        functools.partial(_mix_b_out_body, fox_w),
        grid=(bsz, s_len // tm),
        in_specs=[tok(d), tok(d), tok(fox_w), tok(ym.shape[2]), pl.BlockSpec(wout.shape, lambda b, i: (0, 0))],
        out_specs=tok(d),
        out_shape=jax.ShapeDtypeStruct(x.shape, F32),
        compiler_params=_params("arbitrary", "arbitrary"),
        name="mix_b_out",
    )(x, po, y, ym, wout)


def _final_body(x_ref, po_ref, g_ref, o_ref):
    o_ref[0] = _rms(x_ref[0] + po_ref[0], g_ref[...])


def _final_norm(x, po, g):
    bsz, s_len, d = x.shape
    tm = min(TOKEN_TILE, s_len)
    tok = pl.BlockSpec((1, tm, d), lambda b, i: (b, i, 0))
    return _ordered_call(
        _final_body,
        grid=(bsz, s_len // tm),
        in_specs=[tok, tok, pl.BlockSpec((1, d), lambda b, i: (0, 0))],
        out_specs=tok,
        out_shape=jax.ShapeDtypeStruct(x.shape, F32),
        compiler_params=_params("arbitrary", "arbitrary"),
        name="final_norm",
    )(x, po, g.reshape(1, d))


def kernel(x, mem, g_mix, w_in_a, conv_w, w_in_b, w_out, g_mem, w_mem_kv, g_kv, w_kv, b_f, g_ffn,
           peer_wq, peer_keys, peer_u, peer_v, g_final):
    depth = g_mix.shape[0]
    n_a = w_in_a.shape[0]
    n_mem_heads = w_mem_kv.shape[2] // (2 * HEAD_DIM)
    mkbd, mvbd = _block_diag_mem(_memkv(mem, g_mem, w_mem_kv), n_mem_heads)

    bsz = x.shape[0]
    eighths = sum(r for r, _, _ in BATCH_STREAMS)
    plan, unit = (BATCH_STREAMS, bsz // eighths) if bsz % eighths == 0 else (((1, "sc", "sc"),), bsz)
    tables = [(_pack_rows(peer_u[l]), _pack_rows(peer_v[l]),
               _pack_rows_tc(peer_u[l]), _pack_rows_tc(peer_v[l])) for l in range(depth)]
    states, lo = [], 0
    for r, conv_unit, attn_unit in plan:
        rows = r * unit
        states.append(dict(x=x[lo:lo + rows], po=None, lo=lo, hi=lo + rows,
                           units=(conv_unit == "tc", attn_unit == "tc")))
        lo += rows
    _issue.last = None

    def front(st, layer):
        xs, po = st["x"], st["po"]
        mk, mv = mkbd[layer, st["lo"]:st["hi"]], mvbd[layer, st["lo"]:st["hi"]]
        if layer == n_a:
            st["kv"] = _kv_proj(xs, po, g_kv, w_kv, b_f)
        if layer < n_a:
            xs = _mix_a(xs, po, g_mix[layer], w_in_a[layer], conv_w[layer], mk, mv, w_out[layer])
        else:
            q, ym = _mix_b_in(xs, po, g_mix[layer], w_in_b[layer - n_a], mk, mv)
            y = _fox(q, *st["kv"])
            xs = _mix_b_out(xs, po, y, ym, w_out[layer])
        st["x"] = xs
        st["fin"] = _peer_start(xs, g_ffn[layer], peer_wq[layer], peer_keys[layer], tables[layer],
                                st["units"][layer >= n_a])

    def back(st):
        st["po"] = st["fin"]()

    for st in states:
        st.update(layer=0, next="front", ready=0.0)
    tc_clock = sc_clock = 0.0
    while any(st["layer"] < depth for st in states):
        live = [st for st in states if st["layer"] < depth]
        st = min(live, key=lambda s: (
            max(s["ready"], tc_clock),
            not (s["next"] == "back" and not s["on_tc"]),
            s["on_tc"] if s["next"] == "back" else s["units"][s["layer"] >= n_a]))
        layer, rows = st["layer"], (st["hi"] - st["lo"]) / unit
        tc_clock = max(tc_clock, st["ready"])
        if st["next"] == "front":
            st["on_tc"] = st["units"][layer >= n_a]
            front(st, layer)
            tc_clock += rows * (COST_ROUTE + (COST_MIX_ATTN if layer >= n_a else COST_MIX_CONV)
                                + (COST_TC_UDOT if st["on_tc"] else 0.0))
            st["next"] = "back"
            if st["on_tc"]:
                st["ready"] = tc_clock
            else:
                sc_clock = max(sc_clock, tc_clock) + rows * COST_SC_PASS
                st["ready"] = sc_clock
        else:
            back(st)
            if st["on_tc"]:
                tc_clock += rows * COST_TC_BACK
                st["ready"] = tc_clock
            else:
                sc_clock = max(sc_clock, tc_clock) + rows * COST_SC_PASS
                st["ready"] = sc_clock
            st["next"], st["layer"] = "front", layer + 1
    outs = [_final_norm(st["x"], st["po"], g_final) for st in states]
    _issue.last = None
    return jnp.concatenate(outs, axis=0)
```

```python
import functools

import jax
import jax.numpy as jnp
from jax import lax
from jax.experimental import pallas as pl
from jax.experimental.pallas import tpu as pltpu
from jax.experimental.pallas import tpu_sc as plsc

EPS = 1e-6
HEAD_DIM = 64
CONV_K = 3
PEER_HEADS = 8
PEER_NKEYS = 128
PEER_HALF = 128
PEER_TOPK = 16
PEER_CHUNK = 128
LANES = 128
SUBLANES = 8
TOKEN_TILE = 512
ROUTE_TILE = 256
ATTN_TILE = 512
VMEM_LIMIT = 56 * 1024 * 1024
SC_CORES = 2
SC_SUBCORES = 16
SC_LANES = 16
SC_ROWS = 64
SC_GROUP = 16
PACK_TILE = 1024
TC_RETRIEVE_TILE = 128
TC_ROW_STRIDE = 136
COST_MIX_CONV, COST_MIX_ATTN, COST_ROUTE = 0.06, 0.36, 0.29
COST_SC_PASS, COST_TC_UDOT, COST_TC_BACK = 0.70, 0.63, 0.57
BATCH_STREAMS = ((1, "ssss"), (2, "ssss"), (2, "ssss"), (1, "ttts"), (1, "tttt"), (1, "tttt"))

BF16 = jnp.bfloat16
F32 = jnp.float32


def _params(*sem):
    return pltpu.CompilerParams(dimension_semantics=sem, vmem_limit_bytes=VMEM_LIMIT)


class _IssueOrder:
    last = None


_issue = _IssueOrder()


def _ordered_call(body, *, in_specs, **kw):
    n_in = len(in_specs)

    def run(*args):
        prev = _issue.last
        if prev is None:
            outs = pl.pallas_call(body, in_specs=in_specs, **kw)(*args)
        else:
            def with_token(*refs):
                body(*refs[:n_in], *refs[n_in + 1:])

            outs = pl.pallas_call(with_token, in_specs=list(in_specs) + [pl.BlockSpec(memory_space=pl.ANY)],
                                  **kw)(*args, prev)
        _issue.last = jax.tree.leaves(outs)[0]
        return outs

    return run


def _rms(x, g):
    return x * lax.rsqrt(jnp.mean(x * x, axis=-1, keepdims=True) + EPS) * g


def _mem_attend(qm, mkbd, mvbd, n_heads, n_mem):
    lg = jnp.dot(qm.astype(BF16), mkbd, preferred_element_type=F32) * (HEAD_DIM ** -0.5)
    parts = []
    for h in range(n_heads):
        l = lg[:, h * n_mem:(h + 1) * n_mem]
        e = jnp.exp(l - jnp.max(l, axis=-1, keepdims=True))
        parts.append((e / jnp.sum(e, axis=-1, keepdims=True)).astype(BF16))
    return jnp.dot(jnp.concatenate(parts, axis=1), mvbd, preferred_element_type=F32)


def _memkv_body(mem_ref, g_ref, w_ref, o_ref):
    mn = _rms(mem_ref[0], g_ref[0])
    o_ref[0, 0] = jnp.dot(mn.astype(BF16), w_ref[0], preferred_element_type=F32).astype(BF16)


def _memkv(mem, g_mem, w_mem_kv):
    bsz, n_mem, d = mem.shape
    depth, _, wd = w_mem_kv.shape
    return _ordered_call(
        _memkv_body,
        grid=(depth, bsz),
        in_specs=[
            pl.BlockSpec((1, n_mem, d), lambda l, b: (b, 0, 0)),
            pl.BlockSpec((1, 1, d), lambda l, b: (l, 0, 0)),
            pl.BlockSpec((1, d, wd), lambda l, b: (l, 0, 0)),
        ],
        out_specs=pl.BlockSpec((1, 1, n_mem, wd), lambda l, b: (l, b, 0, 0)),
        out_shape=jax.ShapeDtypeStruct((depth, bsz, n_mem, wd), BF16),
        compiler_params=_params("arbitrary", "arbitrary"),
        name="memkv",
    )(mem, g_mem.reshape(depth, 1, d), w_mem_kv.astype(BF16))


def _block_diag_mem(mkv, n_heads):
    depth, bsz, n_mem, _ = mkv.shape
    r = mkv.reshape(depth, bsz, n_mem, 2, n_heads, HEAD_DIM)
    eye = jnp.eye(n_heads, dtype=mkv.dtype)
    mk = jnp.einsum("lbmhd,hg->lbhdgm", r[:, :, :, 0], eye)
    mv = jnp.einsum("lbmhd,hg->lbhmgd", r[:, :, :, 1], eye)
    hw = n_heads * HEAD_DIM
    return mk.reshape(depth, bsz, hw, n_heads * n_mem), mv.reshape(depth, bsz, n_heads * n_mem, hw)


def _mix_a_body(has_po, conv_ch, n_mem_heads, n_mem, *refs):
    if has_po:
        x_ref, po_ref, g_ref, win_ref, cw_ref, mk_ref, mv_ref, wout_ref, o_ref, cbuf = refs
        x = x_ref[0] + po_ref[0]
    else:
        x_ref, g_ref, win_ref, cw_ref, mk_ref, mv_ref, wout_ref, o_ref, cbuf = refs
        x = x_ref[0]
    tm = x.shape[0]
    h = _rms(x, g_ref[...]).astype(BF16)
    z = jnp.dot(h, win_ref[...], preferred_element_type=F32)
    u = z[:, :conv_ch]
    c_gate = z[:, conv_ch:2 * conv_ch]
    b_gate = z[:, 2 * conv_ch:3 * conv_ch]
    qm = z[:, 3 * conv_ch:]
    cu = c_gate * u

    @pl.when(pl.program_id(1) == 0)
    def _():
        cbuf[0:SUBLANES, :] = jnp.zeros((SUBLANES, conv_ch), F32)

    cbuf[SUBLANES:SUBLANES + tm, :] = cu
    cu1 = cbuf[SUBLANES - 1:SUBLANES - 1 + tm, :]
    cu2 = cbuf[SUBLANES - 2:SUBLANES - 2 + tm, :]
    cw = cw_ref[...]
    y = b_gate * (cw[0:1] * cu2 + cw[1:2] * cu1 + cw[2:3] * cu)
    cbuf[0:SUBLANES, :] = cbuf[tm:tm + SUBLANES, :]

    ymem = _mem_attend(qm, mk_ref[0], mv_ref[0], n_mem_heads, n_mem)
    o_ref[0] = (x
                + jnp.dot(y.astype(BF16), wout_ref[0:conv_ch, :], preferred_element_type=F32)
                + jnp.dot(ymem.astype(BF16), wout_ref[conv_ch:, :], preferred_element_type=F32))


def _mix_a(x, po, g, w_in, conv_w, mkbd, mvbd, w_out, row0=0):
    bsz, s_len, d = mkbd.shape[0], x.shape[1], x.shape[2]
    tm = min(TOKEN_TILE, s_len)
    conv_ch = conv_w.shape[1]
    mem_w = w_in.shape[1] - 3 * conv_ch
    n_mem_heads = mem_w // HEAD_DIM
    n_mem = mkbd.shape[2] // n_mem_heads
    tok = pl.BlockSpec((1, tm, d), lambda b, i: (b, i, 0))
    full = lambda a: pl.BlockSpec(a.shape, lambda b, i: (0,) * a.ndim)
    per_b = lambda a: pl.BlockSpec((1,) + a.shape[1:], lambda b, i: (b, 0, 0))
    g2 = g.reshape(1, d)
    win = w_in.astype(BF16)
    wout = w_out.astype(BF16)
    args = [x] + ([po] if po is not None else []) + [g2, win, conv_w, mkbd, mvbd, wout]
    tok_x = pl.BlockSpec((1, tm, d), lambda b, i: (b + row0, i, 0))
    specs = [tok_x] + ([tok] if po is not None else []) + [full(g2), full(win), full(conv_w), per_b(mkbd), per_b(mvbd), full(wout)]
    return _ordered_call(
        functools.partial(_mix_a_body, po is not None, conv_ch, n_mem_heads, n_mem),
        grid=(bsz, s_len // tm),
        in_specs=specs,
        out_specs=tok,
        out_shape=jax.ShapeDtypeStruct((bsz, s_len, d), F32),
        scratch_shapes=[pltpu.VMEM((tm + SUBLANES, conv_ch), F32)],
        compiler_params=_params("arbitrary", "arbitrary"),
        name="mix_a",
    )(*args)


def _topk_rows(s, k, payload=None):
    n, cols = s.shape
    n_slabs = n // SUBLANES
    sub = lax.broadcasted_iota(jnp.int32, (SUBLANES, cols), 0)
    vals = [s[i * SUBLANES:(i + 1) * SUBLANES] for i in range(n_slabs)]
    rows = [sub + i * SUBLANES for i in range(n_slabs)]
    pays = None if payload is None else [payload[i * SUBLANES:(i + 1) * SUBLANES] for i in range(n_slabs)]
    out_v, out_i = [], []
    for _ in range(k):
        items = [(v, r) + (() if pays is None else (p,)) for v, r, p in zip(vals, rows, pays or rows)]
        while len(items) > 1:
            nxt = []
            for a, b in zip(items[0::2], items[1::2]):
                keep_a = a[0] >= b[0]
                nxt.append(tuple(jnp.where(keep_a, x, y) for x, y in zip(a, b)))
            if len(items) % 2:
                nxt.append(items[-1])
            items = nxt
        best = items[0]
        m = jnp.max(best[0], axis=0, keepdims=True)
        pos = jnp.min(jnp.where(best[0] == m, best[1], n), axis=0, keepdims=True)
        out_v.append(m)
        if pays is None:
            out_i.append(pos)
        else:
            out_i.append(jnp.sum(jnp.where(best[1] == pos, best[2], 0), axis=0, keepdims=True))
        vals = [jnp.where(r == pos, -jnp.inf, v) for v, r in zip(vals, rows)]
    return jnp.concatenate(out_v, axis=0), jnp.concatenate(out_i, axis=0)


def _route_body(x_ref, g_ref, wq_ref, keys_ref, xn_ref, idx_ref, gate_ref, xnb):
    @pl.when(pl.program_id(1) == 0)
    def _():
        xn = _rms(x_ref[...], g_ref[...])
        xn_ref[...] = xn
        xnb[...] = xn.astype(BF16)

    q = jnp.dot(xnb[...], wq_ref[...], preferred_element_type=F32)
    tops = []
    for p in range(2):
        qp = q[:, p * PEER_HALF:(p + 1) * PEER_HALF].astype(BF16)
        st = lax.dot_general(keys_ref[0, p], qp, (((1,), (1,)), ((), ())),
                             preferred_element_type=F32)
        tops.append(_topk_rows(st, PEER_TOPK))
    (v1, i1), (v2, i2) = tops
    cand, cidx, i = [], [], 0
    while i < PEER_TOPK:
        n_j = PEER_TOPK // (i + 1)
        if n_j > 1:
            n_j = min(PEER_TOPK, -(-n_j // SUBLANES) * SUBLANES)
            cand.append(v1[i:i + 1] + v2[0:n_j])
            cidx.append(i1[i:i + 1] * PEER_NKEYS + i2[0:n_j])
            i += 1
        else:
            cand.append(v1[i:] + v2[0:1])
            cidx.append(i1[i:] * PEER_NKEYS + i2[0:1])
            i = PEER_TOPK
    top_s, idx = _topk_rows(jnp.concatenate(cand, axis=0), PEER_TOPK, payload=jnp.concatenate(cidx, axis=0))
    e = jnp.exp(top_s - top_s[0:1])
    idx_ref[...] = idx
    gate_ref[...] = e / jnp.sum(e, axis=0, keepdims=True)


def _peer_route(x2, g, w_q, keys):
    t, d = x2.shape
    tm = min(ROUTE_TILE, t)
    dk = 2 * PEER_HALF
    return _ordered_call(
        _route_body,
        grid=(t // tm, PEER_HEADS),
        in_specs=[
            pl.BlockSpec((tm, d), lambda i, h: (i, 0)),
            pl.BlockSpec((1, d), lambda i, h: (0, 0)),
            pl.BlockSpec((d, dk), lambda i, h: (0, h)),
            pl.BlockSpec((1, 2, PEER_NKEYS, PEER_HALF), lambda i, h: (h, 0, 0, 0)),
        ],
        out_specs=[
            pl.BlockSpec((tm, d), lambda i, h: (i, 0)),
            pl.BlockSpec((PEER_TOPK, tm), lambda i, h: (h, i)),
            pl.BlockSpec((PEER_TOPK, tm), lambda i, h: (h, i)),
        ],
        out_shape=[
            jax.ShapeDtypeStruct((t, d), F32),
            jax.ShapeDtypeStruct((PEER_HEADS * PEER_TOPK, t), jnp.int32),
            jax.ShapeDtypeStruct((PEER_HEADS * PEER_TOPK, t), F32),
        ],
        scratch_shapes=[pltpu.VMEM((tm, d), BF16)],
        compiler_params=_params("arbitrary", "arbitrary"),
        name="peer_route",
    )(x2, g.reshape(1, d), w_q.astype(BF16), keys.astype(BF16))


def _sc_mesh():
    return plsc.VectorSubcoreMesh(core_axis_name="c", subcore_axis_name="s",
                                  num_cores=SC_CORES, num_subcores=SC_SUBCORES)


def _sc_worker_id():
    return lax.axis_index("s") * SC_CORES + lax.axis_index("c")


def _sc_gather_loop(tab_hbm, idx_v, buf, sems, compute):
    per_tok = idx_v.shape[1] // SC_ROWS
    n_batches = SC_GROUP * per_tok

    def gather(j, slot):
        tok = j // per_tok
        row0 = pl.multiple_of((j % per_tok) * SC_ROWS, SC_ROWS)
        return tok, row0, pltpu.make_async_copy(
            tab_hbm.at[idx_v.at[tok, pl.ds(row0, SC_ROWS)]], buf.at[slot], sems.at[slot])

    nbuf = buf.shape[0]
    for b in range(nbuf - 1):
        gather(b, b)[2].start()

    def ring(jj, carry):
        for b in range(nbuf):
            j = jj * nbuf + b
            ahead = j + nbuf - 1

            @pl.when(ahead < n_batches)
            def _():
                gather(ahead, (b + nbuf - 1) % nbuf)[2].start()

            tok, row0, cp = gather(j, b)
            cp.wait()
            compute(tok, row0, buf.at[b])
        return carry

    lax.fori_loop(0, n_batches // nbuf, ring, 0)


def _pack_body(t3_ref, o_ref):
    _pack_tile(t3_ref.at[0], o_ref)


def _pack_tile(t_ref, o_ref):
    x = t_ref[...]
    half = x.shape[1] // 2
    lo = lax.bitcast_convert_type(x[:, :half], jnp.int32)
    hi = lax.bitcast_convert_type(x[:, half:], jnp.int32)
    srl = lax.shift_right_logical
    lo_bits = srl(lo + 0x7FFF + (srl(lo, 16) & 1), 16)
    mag = (hi & 0x7FFFFFFF) + 0x8000
    top = jnp.where(mag >= lo_bits, srl(mag - lo_bits, 16), 0)
    o_ref[...] = (hi & jnp.int32(-2 ** 31)) | (top << 16) | lo_bits


def _unpack_words(w):
    return lax.bitcast_convert_type(w << 16, F32), lax.bitcast_convert_type(w, F32)


def _pack_rows(tabs, layer=0):
    if tabs.ndim == 2:
        tabs = tabs[None]
    _, n, d = tabs.shape
    rows = min(n, PACK_TILE)
    return pl.pallas_call(
        _pack_body, grid=(n // rows,),
        in_specs=[pl.BlockSpec((1, rows, d), lambda i: (layer, i, 0))],
        out_specs=pl.BlockSpec((rows, d // 2), lambda i: (i, 0)),
        out_shape=jax.ShapeDtypeStruct((n, d // 2), jnp.int32),
        compiler_params=_params("arbitrary"), name="pack_rows",
    )(tabs)


def _sc_group_loop(n_groups, stage, work):
    for cp in stage(0, 0):
        cp.start()

    def group(g, carry):
        slot = g % 2
        for cp in stage(g, slot):
            cp.wait()

        @pl.when(g + 1 < n_groups)
        def _():
            for cp in stage(g + 1, 1 - slot):
                cp.start()

        work(g, slot)
        return carry

    lax.fori_loop(0, n_groups, group, 0)


def _sc_udot_body(groups_per_worker, u_hbm, idx_hbm, x_hbm, a_hbm, idx_v, x_v, a_v, buf, tr, sems, ssems):
    g0 = _sc_worker_id() * groups_per_worker
    lanes = lax.iota(jnp.int32, SC_LANES)
    half = x_v.shape[2] // 2
    chunks = half // SC_LANES

    def stage(g, slot):
        gi = g0 + g
        return [pltpu.make_async_copy(idx_hbm.at[gi], idx_v.at[slot], ssems.at[slot, 0]),
                pltpu.make_async_copy(x_hbm.at[pl.ds(gi * SC_GROUP, SC_GROUP)], x_v.at[slot], ssems.at[slot, 1])]

    def work(g, slot):
        def compute(tok, row0, rows):
            for eb in range(SC_ROWS // SC_LANES):
                def body(c, accs):
                    x_lo = x_v[slot, tok, pl.ds(c * SC_LANES, SC_LANES)]
                    x_hi = x_v[slot, tok, pl.ds(half + c * SC_LANES, SC_LANES)]
                    out = []
                    for e in range(SC_LANES):
                        lo, hi = _unpack_words(rows[eb * SC_LANES + e, pl.ds(c * SC_LANES, SC_LANES)])
                        out.append(accs[e] + lo * x_lo + hi * x_hi)
                    return tuple(out)

                accs = lax.fori_loop(0, chunks, body, tuple(jnp.zeros((SC_LANES,), F32) for _ in range(SC_LANES)))
                for e in range(SC_LANES):
                    tr[e, :] = accs[e]
                tot = jnp.zeros((SC_LANES,), F32)
                for c in range(SC_LANES):
                    tot = tot + plsc.load_gather(tr, [lanes, jnp.full((SC_LANES,), c, jnp.int32)])
                a_v[tok, pl.ds(row0 + eb * SC_LANES, SC_LANES)] = tot

        _sc_gather_loop(u_hbm, idx_v.at[slot], buf, sems, compute)
        pltpu.sync_copy(a_v, a_hbm.at[g0 + g])

    _sc_group_loop(groups_per_worker, stage, work)


def _sc_vsum_body(groups_per_worker, v_hbm, idx_hbm, w_hbm, o_hbm, idx_v, w_v, o_v, buf, sems, ssems):
    g0 = _sc_worker_id() * groups_per_worker
    d = o_v.shape[1]
    words_per_pass = SC_LANES // 2
    n_passes = d // (2 * SC_LANES * words_per_pass)

    def stage(g, slot):
        gi = g0 + g
        return [pltpu.make_async_copy(idx_hbm.at[gi], idx_v.at[slot], ssems.at[slot, 0]),
                pltpu.make_async_copy(w_hbm.at[gi], w_v.at[slot], ssems.at[slot, 1])]

    def work(g, slot):
        slot_v = jnp.full((SC_LANES,), slot, jnp.int32)

        def compute(tok, row0, rows):
            tok_v = jnp.full((SC_LANES,), tok, jnp.int32)
            for dg in range(n_passes):
                def body(r, accs):
                    wk = plsc.load_gather(w_v, [slot_v, tok_v, jnp.full((SC_LANES,), row0 + r, jnp.int32)])
                    out = []
                    for cc in range(words_per_pass):
                        lo, hi = _unpack_words(rows[r, pl.ds((dg * words_per_pass + cc) * SC_LANES, SC_LANES)])
                        out += [accs[2 * cc] + wk * lo, accs[2 * cc + 1] + wk * hi]
                    return tuple(out)

                accs = lax.fori_loop(0, SC_ROWS, body, tuple(jnp.zeros((SC_LANES,), F32) for _ in range(SC_LANES)))
                for cc in range(SC_LANES):
                    word0 = (dg * words_per_pass + cc // 2) * SC_LANES
                    sl = (tok, pl.ds((cc % 2) * (d // 2) + word0, SC_LANES))
                    o_v[sl] = o_v[sl] + accs[cc]

        def zero(i, c):
            o_v[i // (d // SC_LANES), pl.ds((i % (d // SC_LANES)) * SC_LANES, SC_LANES)] = jnp.zeros((SC_LANES,), F32)
            return c

        lax.fori_loop(0, SC_GROUP * (d // SC_LANES), zero, 0)
        _sc_gather_loop(v_hbm, idx_v.at[slot], buf, sems, compute)
        pltpu.sync_copy(o_v, o_hbm.at[pl.ds((g0 + g) * SC_GROUP, SC_GROUP)])

    _sc_group_loop(groups_per_worker, stage, work)


def _gate_body(a_ref, g_ref, w_ref):
    a = a_ref[...]
    w_ref[...] = g_ref[...] * (0.5 * a * (1.0 + lax.erf(a * (2.0 ** -0.5))))


def _peer_retrieve(xn, idx, gate, u_tab, v_tab):
    t, d = xn.shape
    hk = idx.shape[1]
    n_groups = t // SC_GROUP
    groups_per_worker = n_groups // (SC_CORES * SC_SUBCORES)
    idx3 = idx.reshape(n_groups, SC_GROUP, hk)
    row_buf = pltpu.VMEM((2, SC_ROWS, d // 2), jnp.int32)
    sems = pltpu.SemaphoreType.DMA((2,))
    stage_sems = pltpu.SemaphoreType.DMA((2, 2))
    cost = pl.CostEstimate(flops=2 * t * hk * d, transcendentals=0,
                           bytes_accessed=2 * t * hk * d + 4 * (t * d + 2 * t * hk))

    a3 = pl.kernel(
        functools.partial(_sc_udot_body, groups_per_worker),
        out_type=jax.ShapeDtypeStruct((n_groups, SC_GROUP, hk), F32),
        mesh=_sc_mesh(),
        scratch_types=[pltpu.VMEM((2, SC_GROUP, hk), jnp.int32), pltpu.VMEM((2, SC_GROUP, d), F32),
                       pltpu.VMEM((SC_GROUP, hk), F32), row_buf, pltpu.VMEM((SC_LANES, SC_LANES), F32), sems,
                       stage_sems],
        compiler_params=pltpu.CompilerParams(needs_layout_passes=False),
        cost_estimate=cost,
        name="peer_udot",
    )(u_tab, idx3, xn)

    def finish():
        g3 = gate.reshape(n_groups, SC_GROUP, hk)
        gb = min(n_groups, 256)
        blk = pl.BlockSpec((gb, SC_GROUP, hk), lambda i: (i, 0, 0))
        w3 = _ordered_call(
            _gate_body, grid=(n_groups // gb,), in_specs=[blk, blk], out_specs=blk,
            out_shape=jax.ShapeDtypeStruct(a3.shape, F32), compiler_params=_params("arbitrary"), name="peer_gate",
        )(a3, g3)
        return pl.kernel(
            functools.partial(_sc_vsum_body, groups_per_worker),
            out_type=jax.ShapeDtypeStruct((t, d), F32),
            mesh=_sc_mesh(),
            scratch_types=[pltpu.VMEM((2, SC_GROUP, hk), jnp.int32), pltpu.VMEM((2, SC_GROUP, hk), F32),
                           pltpu.VMEM((SC_GROUP, d), F32), row_buf, sems, stage_sems],
            compiler_params=pltpu.CompilerParams(needs_layout_passes=False),
            cost_estimate=cost,
            name="peer_vsum",
        )(v_tab, idx3, w3)

    return finish


def _pack_rows_tc(words):
    n, half = words.shape
    return words.reshape(n * (half // LANES), LANES)


def _tc_gather_rows(tab_ref, idx_ref, i, ubuf, n_rows, lines):
    for k in range(n_rows):
        line0 = pl.multiple_of(idx_ref[i, k], lines)
        ubuf[pl.ds(k, lines, stride=TC_ROW_STRIDE), :] = tab_ref[pl.ds(line0, lines), :]


def _tc_halves(ubuf, jb, n_rows):
    return _unpack_words(ubuf[jb * TC_ROW_STRIDE:jb * TC_ROW_STRIDE + n_rows, :])


def _tc_udot_body(tab_ref, idx_ref, x_ref, o_ref, ubuf_a, ubuf_b):
    tt, d = x_ref.shape
    n_rows = o_ref.shape[0]
    lines = d // 2 // LANES
    lane = lax.broadcasted_iota(jnp.int32, (n_rows, tt), 1)

    def group(g, out):
        for j in range(SUBLANES):
            i = g * SUBLANES + j
            ubuf = (ubuf_a, ubuf_b)[j % 2]
            _tc_gather_rows(tab_ref, idx_ref, i, ubuf, n_rows, lines)
            xrow = x_ref[pl.ds(i, 1), :]
            acc = jnp.zeros((n_rows, LANES), F32)
            for jb in range(lines):
                lo, hi = _tc_halves(ubuf, jb, n_rows)
                acc = (acc + lo * xrow[:, jb * LANES:(jb + 1) * LANES]
                       + hi * xrow[:, d // 2 + jb * LANES:d // 2 + (jb + 1) * LANES])
            out = jnp.where(lane == i, jnp.sum(acc, axis=1, keepdims=True), out)
        return out

    o_ref[...] = lax.fori_loop(0, tt // SUBLANES, group, jnp.zeros((n_rows, tt), F32))


def _tc_vsum_body(tab_ref, idx_ref, w_ref, o_ref, ubuf_a, ubuf_b):
    tt, d = o_ref.shape
    n_rows = w_ref.shape[0]
    lines = d // 2 // LANES
    lane = lax.broadcasted_iota(jnp.int32, (n_rows, tt), 1)

    def group(g, carry):
        rows = []
        for j in range(SUBLANES):
            i = g * SUBLANES + j
            ubuf = (ubuf_a, ubuf_b)[j % 2]
            _tc_gather_rows(tab_ref, idx_ref, i, ubuf, n_rows, lines)
            col = jnp.sum(jnp.where(lane == i, w_ref[...], 0.0), axis=1, keepdims=True)
            halves = [_tc_halves(ubuf, jb, n_rows) for jb in range(lines)]
            rows.append(jnp.concatenate(
                [jnp.sum(lo * col, axis=0, keepdims=True) for lo, _ in halves]
                + [jnp.sum(hi * col, axis=0, keepdims=True) for _, hi in halves], axis=1))
        o_ref[pl.ds(pl.multiple_of(g * SUBLANES, SUBLANES), SUBLANES), :] = jnp.concatenate(rows, axis=0)
        return carry

    lax.fori_loop(0, tt // SUBLANES, group, 0)


def _tc_table_specs(tab4, hk, tt):
    lines_total = tab4.shape[0]
    return [pl.BlockSpec((lines_total, LANES), lambda i: (0, 0), pipeline_mode=pl.Buffered(1)),
            pl.BlockSpec((tt, hk), lambda i: (i, 0), memory_space=pltpu.SMEM)]


def _tc_udot(tab4, idx, x):
    t, d = x.shape
    hk = idx.shape[1]
    tt = min(TC_RETRIEVE_TILE, t)
    return _ordered_call(
        _tc_udot_body, grid=(t // tt,),
        in_specs=_tc_table_specs(tab4, hk, tt) + [pl.BlockSpec((tt, d), lambda i: (i, 0))],
        out_specs=pl.BlockSpec((hk, tt), lambda i: (0, i)),
        out_shape=jax.ShapeDtypeStruct((hk, t), F32),
        scratch_shapes=[pltpu.VMEM((d // 2 // LANES * TC_ROW_STRIDE, LANES), jnp.int32)] * 2,
        compiler_params=_params("arbitrary"), name="tc_udot",
    )(tab4, idx, x)


def _tc_vsum(tab4, idx, w_t):
    hk, t = w_t.shape
    d = tab4.shape[1] * 2 * (tab4.shape[0] // PEER_NKEYS ** 2)
    tt = min(TC_RETRIEVE_TILE, t)
    return _ordered_call(
        _tc_vsum_body, grid=(t // tt,),
        in_specs=_tc_table_specs(tab4, hk, tt) + [pl.BlockSpec((hk, tt), lambda i: (0, i))],
        out_specs=pl.BlockSpec((tt, d), lambda i: (i, 0)),
        out_shape=jax.ShapeDtypeStruct((t, d), F32),
        scratch_shapes=[pltpu.VMEM((d // 2 // LANES * TC_ROW_STRIDE, LANES), jnp.int32)] * 2,
        compiler_params=_params("arbitrary"), name="tc_vsum",
    )(tab4, idx, w_t)


def _peer_retrieve_tc(xn, idx_t, gate_t, u_tab4, v_tab4):
    hk, t = idx_t.shape
    idx = idx_t.T * (u_tab4.shape[0] // PEER_NKEYS ** 2)
    a_t = _tc_udot(u_tab4, idx, xn)

    def finish():
        tt = min(TC_RETRIEVE_TILE, t)
        blk = pl.BlockSpec((hk, tt), lambda i: (0, i))
        w_t = _ordered_call(
            _gate_body, grid=(t // tt,), in_specs=[blk, blk], out_specs=blk,
            out_shape=jax.ShapeDtypeStruct((hk, t), F32), compiler_params=_params("arbitrary"), name="tc_gate",
        )(a_t, gate_t)
        return _tc_vsum(v_tab4, idx, w_t)

    return finish


def _peer_start(x, g, w_q, keys, tables, on_tc):
    bsz, s_len, d = x.shape
    xn, idx_t, gate_t = _peer_route(x.reshape(bsz * s_len, d), g, w_q, keys)
    if on_tc:
        fin = _peer_retrieve_tc(xn, idx_t, gate_t, tables[2], tables[3])
    else:
        fin = _peer_retrieve(xn, idx_t.T, gate_t.T, tables[0], tables[1])
    return lambda: fin().reshape(bsz, s_len, d)


def _kv_body(fox_w, x_ref, po_ref, g_ref, w_ref, bf_ref, k_ref, v_ref, c_ref, ct_ref, carry):
    x = x_ref[0] + po_ref[0]
    tm = x.shape[0]
    hs = _rms(x, g_ref[...]).astype(BF16)
    z = jnp.dot(hs, w_ref[...], preferred_element_type=F32)
    k_ref[0] = z[:, :fox_w].astype(BF16)
    v_ref[0] = z[:, fox_w:2 * fox_w].astype(BF16)
    zf = z[:, 2 * fox_w:] + bf_ref[...]
    logf = jnp.minimum(zf, 0.0) - jnp.log1p(jnp.exp(-jnp.abs(zf)))

    @pl.when(pl.program_id(1) == 0)
    def _():
        carry[...] = jnp.zeros(carry.shape, F32)

    r = lax.broadcasted_iota(jnp.int32, (tm, tm), 0)
    c = lax.broadcasted_iota(jnp.int32, (tm, tm), 1)
    tri = (c <= r).astype(F32)
    cs = jnp.dot(tri, logf, preferred_element_type=F32, precision=lax.Precision.HIGHEST) + carry[...]
    carry[...] = cs[tm - 1:tm, :]
    c_ref[0] = cs
    ct_ref[0] = cs.T[0:ct_ref.shape[1], :]


def _kv_proj(x, po, g, w_kv, b_f):
    bsz, s_len, d = x.shape
    tm = min(TOKEN_TILE, s_len)
    n_heads = b_f.shape[0]
    fox_w = (w_kv.shape[1] - n_heads) // 2
    w = jnp.pad(w_kv, ((0, 0), (0, LANES - n_heads))).astype(BF16)
    bf = jnp.pad(b_f, (0, LANES - n_heads)).reshape(1, LANES)
    hp = -(-n_heads // SUBLANES) * SUBLANES
    tok = lambda w_, dt: (pl.BlockSpec((1, tm, w_), lambda b, i: (b, i, 0)), jax.ShapeDtypeStruct((bsz, s_len, w_), dt))
    (ks, ksh), (vs, vsh), (cs, csh) = tok(fox_w, BF16), tok(fox_w, BF16), tok(LANES, F32)
    return _ordered_call(
        functools.partial(_kv_body, fox_w),
        grid=(bsz, s_len // tm),
        in_specs=[
            pl.BlockSpec((1, tm, d), lambda b, i: (b, i, 0)),
            pl.BlockSpec((1, tm, d), lambda b, i: (b, i, 0)),
            pl.BlockSpec((1, d), lambda b, i: (0, 0)),
            pl.BlockSpec(w.shape, lambda b, i: (0, 0)),
            pl.BlockSpec((1, LANES), lambda b, i: (0, 0)),
        ],
        out_specs=[ks, vs, cs, pl.BlockSpec((1, hp, tm), lambda b, i: (b, 0, i))],
        out_shape=[ksh, vsh, csh, jax.ShapeDtypeStruct((bsz, hp, s_len), F32)],
        scratch_shapes=[pltpu.VMEM((1, LANES), F32)],
        compiler_params=_params("arbitrary", "arbitrary"),
        name="kv_proj",
    )(x, po, g.reshape(1, d), w, bf)


def _mix_b_in_body(fox_w, n_mem_heads, n_mem, x_ref, po_ref, g_ref, win_ref, mk_ref, mv_ref, q_ref, ym_ref):
    x = x_ref[0] + po_ref[0]
    h = _rms(x, g_ref[...]).astype(BF16)
    z = jnp.dot(h, win_ref[...], preferred_element_type=F32)
    q_ref[0] = (z[:, :fox_w] * (HEAD_DIM ** -0.5)).astype(BF16)
    ym_ref[0] = _mem_attend(z[:, fox_w:], mk_ref[0], mv_ref[0], n_mem_heads, n_mem).astype(BF16)


def _mix_b_in(x, po, g, w_in, mkbd, mvbd):
    bsz, s_len, d = x.shape
    tm = min(TOKEN_TILE, s_len)
    mem_w = mvbd.shape[2]
    fox_w = w_in.shape[1] - mem_w
    n_mem_heads = mem_w // HEAD_DIM
    n_mem = mkbd.shape[2] // n_mem_heads
    win = w_in.astype(BF16)
    tok = lambda w_: pl.BlockSpec((1, tm, w_), lambda b, i: (b, i, 0))
    per_b = lambda a: pl.BlockSpec((1,) + a.shape[1:], lambda b, i: (b, 0, 0))
    return _ordered_call(
        functools.partial(_mix_b_in_body, fox_w, n_mem_heads, n_mem),
        grid=(bsz, s_len // tm),
        in_specs=[tok(d), tok(d), pl.BlockSpec((1, d), lambda b, i: (0, 0)),
                  pl.BlockSpec(win.shape, lambda b, i: (0, 0)), per_b(mkbd), per_b(mvbd)],
        out_specs=[tok(fox_w), tok(mem_w)],
        out_shape=[jax.ShapeDtypeStruct((bsz, s_len, fox_w), BF16),
                   jax.ShapeDtypeStruct((bsz, s_len, mem_w), BF16)],
        compiler_params=_params("arbitrary", "arbitrary"),
        name="mix_b_in",
    )(x, po, g.reshape(1, d), win, mkbd, mvbd)


def _fox_body(n_heads, q_ref, k_ref, v_ref, c_ref, ct_ref, o_ref, m_scr, l_scr, cq_scr, acc):
    qi, ki = pl.program_id(1), pl.program_id(2)
    tq, tk = q_ref.shape[1], k_ref.shape[1]

    def wide(a):
        return jnp.concatenate([a] * (tk // LANES), axis=1)

    @pl.when(ki == 0)
    def _():
        m_scr[...] = jnp.full(m_scr.shape, -jnp.inf, F32)
        l_scr[...] = jnp.zeros(l_scr.shape, F32)
        acc[...] = jnp.zeros(acc.shape, F32)
        cq = c_ref[0]
        for h in range(n_heads):
            cq_scr[h] = jnp.broadcast_to(cq[:, h:h + 1], (tq, LANES))

    def step(masked):
        if masked:
            keep = (lax.broadcasted_iota(jnp.int32, (tq, tk), 0)
                    >= lax.broadcasted_iota(jnp.int32, (tq, tk), 1))
        for h in range(n_heads):
            sl = slice(h * HEAD_DIM, (h + 1) * HEAD_DIM)
            s = lax.dot_general(q_ref[0, :, sl], k_ref[0, :, sl], (((1,), (1,)), ((), ())),
                                preferred_element_type=F32)
            s = s + (wide(cq_scr[h]) - ct_ref[0, h:h + 1, :])
            if masked:
                s = jnp.where(keep, s, -jnp.inf)
            m_old = m_scr[h]
            m_new = jnp.maximum(m_old, jnp.max(s, axis=-1, keepdims=True))
            alpha = jnp.exp(m_old - m_new)
            p = jnp.exp(s - wide(m_new))
            l_scr[h] = alpha * l_scr[h] + jnp.sum(p, axis=-1, keepdims=True)
            acc[h] = alpha[:, :HEAD_DIM] * acc[h] + jnp.dot(p.astype(BF16), v_ref[0, :, sl],
                                                           preferred_element_type=F32)
            m_scr[h] = m_new

    @pl.when(ki < qi)
    def _():
        step(False)

    @pl.when(ki == qi)
    def _():
        step(True)
        for h in range(n_heads):
            sl = slice(h * HEAD_DIM, (h + 1) * HEAD_DIM)
            o_ref[0, :, sl] = (acc[h] / l_scr[h][:, :HEAD_DIM]).astype(BF16)


def _fox(q, k, v, c, ct):
    bsz, s_len, w = q.shape
    n_heads = w // HEAD_DIM
    t = min(ATTN_TILE, s_len)
    nb = s_len // t
    return _ordered_call(
        functools.partial(_fox_body, n_heads),
        grid=(bsz, nb, nb),
        in_specs=[
            pl.BlockSpec((1, t, w), lambda b, i, j: (b, i, 0)),
            pl.BlockSpec((1, t, w), lambda b, i, j: (b, jnp.minimum(i, j), 0)),
            pl.BlockSpec((1, t, w), lambda b, i, j: (b, jnp.minimum(i, j), 0)),
            pl.BlockSpec((1, t, c.shape[2]), lambda b, i, j: (b, i, 0)),
            pl.BlockSpec((1, ct.shape[1], t), lambda b, i, j: (b, 0, jnp.minimum(i, j))),
        ],
        out_specs=pl.BlockSpec((1, t, w), lambda b, i, j: (b, i, 0)),
        out_shape=jax.ShapeDtypeStruct((bsz, s_len, w), BF16),
        scratch_shapes=[pltpu.VMEM((n_heads, t, LANES), F32), pltpu.VMEM((n_heads, t, LANES), F32),
                        pltpu.VMEM((n_heads, t, LANES), F32), pltpu.VMEM((n_heads, t, HEAD_DIM), F32)],
        compiler_params=_params("arbitrary", "arbitrary", "arbitrary"),
        name="fox",
    )(q, k, v, c, ct)


def _mix_b_out_body(fox_w, x_ref, po_ref, y_ref, ym_ref, wout_ref, o_ref):
    o_ref[0] = (x_ref[0] + po_ref[0]
                + jnp.dot(y_ref[0], wout_ref[0:fox_w, :], preferred_element_type=F32)
                + jnp.dot(ym_ref[0], wout_ref[fox_w:, :], preferred_element_type=F32))


def _mix_b_out(x, po, y, ym, w_out):
    bsz, s_len, d = x.shape
    tm = min(TOKEN_TILE, s_len)
    fox_w = y.shape[2]
    wout = w_out.astype(BF16)
    tok = lambda w_: pl.BlockSpec((1, tm, w_), lambda b, i: (b, i, 0))
    return _ordered_call(
        functools.partial(_mix_b_out_body, fox_w),
        grid=(bsz, s_len // tm),
        in_specs=[tok(d), tok(d), tok(fox_w), tok(ym.shape[2]), pl.BlockSpec(wout.shape, lambda b, i: (0, 0))],
        out_specs=tok(d),
        out_shape=jax.ShapeDtypeStruct(x.shape, F32),
        compiler_params=_params("arbitrary", "arbitrary"),
        name="mix_b_out",
    )(x, po, y, ym, wout)


def _final_body(x_ref, po_ref, g_ref, o_ref):
    o_ref[0] = _rms(x_ref[0] + po_ref[0], g_ref[...])


def _final_norm(x, po, g):
    bsz, s_len, d = x.shape
    tm = min(TOKEN_TILE, s_len)
    tok = pl.BlockSpec((1, tm, d), lambda b, i: (b, i, 0))
    return _ordered_call(
        _final_body,
        grid=(bsz, s_len // tm),
        in_specs=[tok, tok, pl.BlockSpec((1, d), lambda b, i: (0, 0))],
        out_specs=tok,
        out_shape=jax.ShapeDtypeStruct(x.shape, F32),
        compiler_params=_params("arbitrary", "arbitrary"),
        name="final_norm",
    )(x, po, g.reshape(1, d))


def kernel(x, mem, g_mix, w_in_a, conv_w, w_in_b, w_out, g_mem, w_mem_kv, g_kv, w_kv, b_f, g_ffn,
           peer_wq, peer_keys, peer_u, peer_v, g_final):
    depth = g_mix.shape[0]
    n_a = w_in_a.shape[0]
    n_mem_heads = w_mem_kv.shape[2] // (2 * HEAD_DIM)
    mkbd, mvbd = _block_diag_mem(_memkv(mem, g_mem, w_mem_kv), n_mem_heads)

    bsz = x.shape[0]
    eighths = sum(r for r, _ in BATCH_STREAMS)
    plan, unit = (BATCH_STREAMS, bsz // eighths) if bsz % eighths == 0 else (((1, "s"),), bsz)
    tables = []
    for l in range(depth):
        u_pk, v_pk = _pack_rows(peer_u, l), _pack_rows(peer_v, l)
        tables.append((u_pk, v_pk, _pack_rows_tc(u_pk), _pack_rows_tc(v_pk)))
    states, lo = [], 0
    for r, letters in plan:
        rows = r * unit
        states.append(dict(x=x[lo:lo + rows], po=None, lo=lo, hi=lo + rows,
                           units=[letters[min(l, len(letters) - 1)] == "t" for l in range(depth + 1)]))
        lo += rows
    _issue.last = None

    def front(st, layer):
        xs, po = st["x"], st["po"]
        mk, mv = mkbd[layer, st["lo"]:st["hi"]], mvbd[layer, st["lo"]:st["hi"]]
        if layer == n_a:
            st["kv"] = _kv_proj(xs, po, g_kv, w_kv, b_f)
        if layer < n_a and layer == 0:
            xs = _mix_a(x, None, g_mix[0], w_in_a[0], conv_w[0], mk, mv, w_out[0], row0=st["lo"])
        elif layer < n_a:
            xs = _mix_a(xs, po, g_mix[layer], w_in_a[layer], conv_w[layer], mk, mv, w_out[layer])
        else:
            q, ym = _mix_b_in(xs, po, g_mix[layer], w_in_b[layer - n_a], mk, mv)
            y = _fox(q, *st["kv"])
            xs = _mix_b_out(xs, po, y, ym, w_out[layer])
        st["x"] = xs
        st["fin"] = _peer_start(xs, g_ffn[layer], peer_wq[layer], peer_keys[layer], tables[layer],
                                st["units"][layer])

    def back(st):
        st["po"] = st["fin"]()

    for st in states:
        st.update(layer=0, next="front", ready=0.0)
    tc_clock = sc_clock = 0.0
    while any(st["layer"] < depth for st in states):
        live = [st for st in states if st["layer"] < depth]
        st = min(live, key=lambda s: (
            max(s["ready"], tc_clock),
            not (s["next"] == "back" and not s["on_tc"]),
            s["on_tc"] if s["next"] == "back" else s["units"][s["layer"]]))
        layer, rows = st["layer"], (st["hi"] - st["lo"]) / unit
        tc_clock = max(tc_clock, st["ready"])
        if st["next"] == "front":
            st["on_tc"] = st["units"][layer]
            front(st, layer)
            tc_clock += rows * (COST_ROUTE + (COST_MIX_ATTN if layer >= n_a else COST_MIX_CONV)
                                + (COST_TC_UDOT if st["on_tc"] else 0.0))
            st["next"] = "back"
            if st["on_tc"]:
                st["ready"] = tc_clock
            else:
                sc_clock = max(sc_clock, tc_clock) + rows * COST_SC_PASS
                st["ready"] = sc_clock
        else:
            back(st)
            if st["on_tc"]:
                tc_clock += rows * COST_TC_BACK
                st["ready"] = tc_clock
            else:
                sc_clock = max(sc_clock, tc_clock) + rows * COST_SC_PASS
                st["ready"] = sc_clock
            st["next"], st["layer"] = "front", layer + 1
    outs = [_final_norm(st["x"], st["po"], g_final) for st in states]
    _issue.last = None
    return jnp.concatenate(outs, axis=0)
```

```python
import functools

import jax
import jax.numpy as jnp
from jax import lax
from jax.experimental import pallas as pl
from jax.experimental.pallas import tpu as pltpu
from jax.experimental.pallas import tpu_sc as plsc

EPS = 1e-6
HEAD_DIM = 64
CONV_K = 3
PEER_HEADS = 8
PEER_NKEYS = 128
PEER_HALF = 128
PEER_TOPK = 16
PEER_CHUNK = 128
LANES = 128
SUBLANES = 8
TOKEN_TILE = 512
ROUTE_TILE = 256
ROUTE_HEADS = 4
ATTN_TILE = 512
VMEM_LIMIT = 56 * 1024 * 1024
SC_CORES = 2
SC_SUBCORES = 16
SC_LANES = 16
SC_ROWS = 64
SC_GROUP = 16
PACK_TILE = 1024
TC_RETRIEVE_TILE = 128
TC_ROW_STRIDE = 136
COST_MIX_CONV, COST_MIX_ATTN, COST_ROUTE = 0.06, 0.36, 0.22
COST_SC_PASS, COST_TC_UDOT, COST_TC_BACK = 0.70, 0.63, 0.57
BATCH_STREAMS = ((1, "ssss"), (2, "ssss"), (2, "ssss"), (1, "tttt"), (1, "tttt"), (1, "tttt"))

BF16 = jnp.bfloat16
F32 = jnp.float32


def _params(*sem):
    return pltpu.CompilerParams(dimension_semantics=sem, vmem_limit_bytes=VMEM_LIMIT)


class _IssueOrder:
    last = None


_issue = _IssueOrder()


def _ordered_call(body, *, in_specs, **kw):
    n_in = len(in_specs)

    def run(*args):
        prev = _issue.last
        if prev is None:
            outs = pl.pallas_call(body, in_specs=in_specs, **kw)(*args)
        else:
            def with_token(*refs):
                body(*refs[:n_in], *refs[n_in + 1:])

            outs = pl.pallas_call(with_token, in_specs=list(in_specs) + [pl.BlockSpec(memory_space=pl.ANY)],
                                  **kw)(*args, prev)
        _issue.last = jax.tree.leaves(outs)[0]
        return outs

    return run


def _rms(x, g):
    return x * lax.rsqrt(jnp.mean(x * x, axis=-1, keepdims=True) + EPS) * g


def _mem_attend(qm, mkbd, mvbd, n_heads, n_mem):
    lg = jnp.dot(qm.astype(BF16), mkbd, preferred_element_type=F32) * (HEAD_DIM ** -0.5)
    parts = []
    for h in range(n_heads):
        l = lg[:, h * n_mem:(h + 1) * n_mem]
        e = jnp.exp(l - jnp.max(l, axis=-1, keepdims=True))
        parts.append((e / jnp.sum(e, axis=-1, keepdims=True)).astype(BF16))
    return jnp.dot(jnp.concatenate(parts, axis=1), mvbd, preferred_element_type=F32)


def _memkv_body(mem_ref, g_ref, w_ref, o_ref):
    mn = _rms(mem_ref[0], g_ref[0])
    o_ref[0, 0] = jnp.dot(mn.astype(BF16), w_ref[0], preferred_element_type=F32).astype(BF16)


def _memkv(mem, g_mem, w_mem_kv):
    bsz, n_mem, d = mem.shape
    depth, _, wd = w_mem_kv.shape
    return _ordered_call(
        _memkv_body,
        grid=(depth, bsz),
        in_specs=[
            pl.BlockSpec((1, n_mem, d), lambda l, b: (b, 0, 0)),
            pl.BlockSpec((1, 1, d), lambda l, b: (l, 0, 0)),
            pl.BlockSpec((1, d, wd), lambda l, b: (l, 0, 0)),
        ],
        out_specs=pl.BlockSpec((1, 1, n_mem, wd), lambda l, b: (l, b, 0, 0)),
        out_shape=jax.ShapeDtypeStruct((depth, bsz, n_mem, wd), BF16),
        compiler_params=_params("arbitrary", "arbitrary"),
        name="memkv",
    )(mem, g_mem.reshape(depth, 1, d), w_mem_kv.astype(BF16))


def _block_diag_mem(mkv, n_heads):
    depth, bsz, n_mem, _ = mkv.shape
    r = mkv.reshape(depth, bsz, n_mem, 2, n_heads, HEAD_DIM)
    eye = jnp.eye(n_heads, dtype=mkv.dtype)
    mk = jnp.einsum("lbmhd,hg->lbhdgm", r[:, :, :, 0], eye)
    mv = jnp.einsum("lbmhd,hg->lbhmgd", r[:, :, :, 1], eye)
    hw = n_heads * HEAD_DIM
    return mk.reshape(depth, bsz, hw, n_heads * n_mem), mv.reshape(depth, bsz, n_heads * n_mem, hw)


def _mix_a_body(has_po, conv_ch, n_mem_heads, n_mem, *refs):
    if has_po:
        x_ref, po_ref, g_ref, win_ref, cw_ref, mk_ref, mv_ref, wout_ref, o_ref, cbuf = refs
        x = x_ref[0] + po_ref[0]
    else:
        x_ref, g_ref, win_ref, cw_ref, mk_ref, mv_ref, wout_ref, o_ref, cbuf = refs
        x = x_ref[0]
    tm = x.shape[0]
    h = _rms(x, g_ref[...]).astype(BF16)
    z = jnp.dot(h, win_ref[...], preferred_element_type=F32)
    u = z[:, :conv_ch]
    c_gate = z[:, conv_ch:2 * conv_ch]
    b_gate = z[:, 2 * conv_ch:3 * conv_ch]
    qm = z[:, 3 * conv_ch:]
    cu = c_gate * u

    @pl.when(pl.program_id(1) == 0)
    def _():
        cbuf[0:SUBLANES, :] = jnp.zeros((SUBLANES, conv_ch), F32)

    cbuf[SUBLANES:SUBLANES + tm, :] = cu
    cu1 = cbuf[SUBLANES - 1:SUBLANES - 1 + tm, :]
    cu2 = cbuf[SUBLANES - 2:SUBLANES - 2 + tm, :]
    cw = cw_ref[...]
    y = b_gate * (cw[0:1] * cu2 + cw[1:2] * cu1 + cw[2:3] * cu)
    cbuf[0:SUBLANES, :] = cbuf[tm:tm + SUBLANES, :]

    ymem = _mem_attend(qm, mk_ref[0], mv_ref[0], n_mem_heads, n_mem)
    o_ref[0] = (x
                + jnp.dot(y.astype(BF16), wout_ref[0:conv_ch, :], preferred_element_type=F32)
                + jnp.dot(ymem.astype(BF16), wout_ref[conv_ch:, :], preferred_element_type=F32))


def _mix_a(x, po, g, w_in, conv_w, mkbd, mvbd, w_out, row0=0):
    bsz, s_len, d = mkbd.shape[0], x.shape[1], x.shape[2]
    tm = min(TOKEN_TILE, s_len)
    conv_ch = conv_w.shape[1]
    mem_w = w_in.shape[1] - 3 * conv_ch
    n_mem_heads = mem_w // HEAD_DIM
    n_mem = mkbd.shape[2] // n_mem_heads
    tok = pl.BlockSpec((1, tm, d), lambda b, i: (b, i, 0))
    full = lambda a: pl.BlockSpec(a.shape, lambda b, i: (0,) * a.ndim)
    per_b = lambda a: pl.BlockSpec((1,) + a.shape[1:], lambda b, i: (b, 0, 0))
    g2 = g.reshape(1, d)
    win = w_in.astype(BF16)
    wout = w_out.astype(BF16)
    args = [x] + ([po] if po is not None else []) + [g2, win, conv_w, mkbd, mvbd, wout]
    tok_x = pl.BlockSpec((1, tm, d), lambda b, i: (b + row0, i, 0))
    specs = [tok_x] + ([tok] if po is not None else []) + [full(g2), full(win), full(conv_w), per_b(mkbd), per_b(mvbd), full(wout)]
    return _ordered_call(
        functools.partial(_mix_a_body, po is not None, conv_ch, n_mem_heads, n_mem),
        grid=(bsz, s_len // tm),
        in_specs=specs,
        out_specs=tok,
        out_shape=jax.ShapeDtypeStruct((bsz, s_len, d), F32),
        scratch_shapes=[pltpu.VMEM((tm + SUBLANES, conv_ch), F32)],
        compiler_params=_params("arbitrary", "arbitrary"),
        name="mix_a",
    )(*args)


def _topk_rows(problems, k):
    state = []
    for s, payload in problems:
        n, cols = s.shape
        sub = lax.broadcasted_iota(jnp.int32, (SUBLANES, cols), 0)
        tiles = range(n // SUBLANES)
        state.append(dict(
            n=n, vals=[s[i * SUBLANES:(i + 1) * SUBLANES] for i in tiles], rows=[sub + i * SUBLANES for i in tiles],
            pays=None if payload is None else [payload[i * SUBLANES:(i + 1) * SUBLANES] for i in tiles],
            out_v=[], out_i=[]))
    for _ in range(k):
        for p in state:
            items = [(v, r) + (() if p["pays"] is None else (q,))
                     for v, r, q in zip(p["vals"], p["rows"], p["pays"] or p["rows"])]
            while len(items) > 1:
                nxt = []
                for a, b in zip(items[0::2], items[1::2]):
                    keep_a = a[0] >= b[0]
                    nxt.append(tuple(jnp.where(keep_a, x, y) for x, y in zip(a, b)))
                if len(items) % 2:
                    nxt.append(items[-1])
                items = nxt
            p["best"] = items[0]
        for p in state:
            best = p["best"]
            m = jnp.max(best[0], axis=0, keepdims=True)
            pos = jnp.min(jnp.where(best[0] == m, best[1], p["n"]), axis=0, keepdims=True)
            p["out_v"].append(m)
            if p["pays"] is None:
                p["out_i"].append(pos)
            else:
                p["out_i"].append(jnp.sum(jnp.where(best[1] == pos, best[2], 0), axis=0, keepdims=True))
            p["vals"] = [jnp.where(r == pos, -jnp.inf, v) for v, r in zip(p["vals"], p["rows"])]
    return [(jnp.concatenate(p["out_v"], axis=0), jnp.concatenate(p["out_i"], axis=0)) for p in state]


def _route_body(x_ref, g_ref, wq_ref, keys_ref, xn_ref, idx_ref, gate_ref, xnb):
    @pl.when(pl.program_id(1) == 0)
    def _():
        xn = _rms(x_ref[...], g_ref[...])
        xn_ref[...] = xn
        xnb[...] = xn.astype(BF16)

    n_heads = keys_ref.shape[0]
    q = jnp.dot(xnb[...], wq_ref[...], preferred_element_type=F32)
    scores = []
    for hp in range(2 * n_heads):
        qp = q[:, hp * PEER_HALF:(hp + 1) * PEER_HALF].astype(BF16)
        scores.append((lax.dot_general(keys_ref[hp // 2, hp % 2], qp, (((1,), (1,)), ((), ())),
                                       preferred_element_type=F32), None))
    tops = _topk_rows(scores, PEER_TOPK)
    cands = []
    for h in range(n_heads):
        (v1, i1), (v2, i2) = tops[2 * h], tops[2 * h + 1]
        cand, cidx, i = [], [], 0
        while i < PEER_TOPK:
            n_j = PEER_TOPK // (i + 1)
            if n_j > 1:
                n_j = min(PEER_TOPK, -(-n_j // SUBLANES) * SUBLANES)
                cand.append(v1[i:i + 1] + v2[0:n_j])
                cidx.append(i1[i:i + 1] * PEER_NKEYS + i2[0:n_j])
                i += 1
            else:
                cand.append(v1[i:] + v2[0:1])
                cidx.append(i1[i:] * PEER_NKEYS + i2[0:1])
                i = PEER_TOPK
        cands.append((jnp.concatenate(cand, axis=0), jnp.concatenate(cidx, axis=0)))
    for h, (top_s, idx) in enumerate(_topk_rows(cands, PEER_TOPK)):
        e = jnp.exp(top_s - top_s[0:1])
        idx_ref[h * PEER_TOPK:(h + 1) * PEER_TOPK, :] = idx
        gate_ref[h * PEER_TOPK:(h + 1) * PEER_TOPK, :] = e / jnp.sum(e, axis=0, keepdims=True)


def _peer_route(x2, g, w_q, keys):
    t, d = x2.shape
    tm = min(ROUTE_TILE, t)
    hs = ROUTE_HEADS
    dk = 2 * PEER_HALF * hs
    return _ordered_call(
        _route_body,
        grid=(t // tm, PEER_HEADS // hs),
        in_specs=[
            pl.BlockSpec((tm, d), lambda i, h: (i, 0)),
            pl.BlockSpec((1, d), lambda i, h: (0, 0)),
            pl.BlockSpec((d, dk), lambda i, h: (0, h)),
            pl.BlockSpec((hs, 2, PEER_NKEYS, PEER_HALF), lambda i, h: (h, 0, 0, 0)),
        ],
        out_specs=[
            pl.BlockSpec((tm, d), lambda i, h: (i, 0)),
            pl.BlockSpec((hs * PEER_TOPK, tm), lambda i, h: (h, i)),
            pl.BlockSpec((hs * PEER_TOPK, tm), lambda i, h: (h, i)),
        ],
        out_shape=[
            jax.ShapeDtypeStruct((t, d), F32),
            jax.ShapeDtypeStruct((PEER_HEADS * PEER_TOPK, t), jnp.int32),
            jax.ShapeDtypeStruct((PEER_HEADS * PEER_TOPK, t), F32),
        ],
        scratch_shapes=[pltpu.VMEM((tm, d), BF16)],
        compiler_params=_params("arbitrary", "arbitrary"),
        name="peer_route",
    )(x2, g.reshape(1, d), w_q.astype(BF16), keys.astype(BF16))


def _sc_mesh():
    return plsc.VectorSubcoreMesh(core_axis_name="c", subcore_axis_name="s",
                                  num_cores=SC_CORES, num_subcores=SC_SUBCORES)


def _sc_worker_id():
    return lax.axis_index("s") * SC_CORES + lax.axis_index("c")


def _sc_gather_loop(tab_hbm, idx_v, buf, sems, compute):
    per_tok = idx_v.shape[1] // SC_ROWS
    n_batches = SC_GROUP * per_tok

    def gather(j, slot):
        tok = j // per_tok
        row0 = pl.multiple_of((j % per_tok) * SC_ROWS, SC_ROWS)
        return tok, row0, pltpu.make_async_copy(
            tab_hbm.at[idx_v.at[tok, pl.ds(row0, SC_ROWS)]], buf.at[slot], sems.at[slot])

    nbuf = buf.shape[0]
    for b in range(nbuf - 1):
        gather(b, b)[2].start()

    def ring(jj, carry):
        for b in range(nbuf):
            j = jj * nbuf + b
            ahead = j + nbuf - 1

            @pl.when(ahead < n_batches)
            def _():
                gather(ahead, (b + nbuf - 1) % nbuf)[2].start()

            tok, row0, cp = gather(j, b)
            cp.wait()
            compute(tok, row0, buf.at[b])
        return carry

    lax.fori_loop(0, n_batches // nbuf, ring, 0)


def _pack_body(t3_ref, o_ref):
    _pack_tile(t3_ref.at[0], o_ref)


def _pack_tile(t_ref, o_ref):
    x = t_ref[...]
    half = x.shape[1] // 2
    lo = lax.bitcast_convert_type(x[:, :half], jnp.int32)
    hi = lax.bitcast_convert_type(x[:, half:], jnp.int32)
    srl = lax.shift_right_logical
    lo_bits = srl(lo + 0x7FFF + (srl(lo, 16) & 1), 16)
    mag = (hi & 0x7FFFFFFF) + 0x8000
    top = jnp.where(mag >= lo_bits, srl(mag - lo_bits, 16), 0)
    o_ref[...] = (hi & jnp.int32(-2 ** 31)) | (top << 16) | lo_bits


def _unpack_words(w):
    return lax.bitcast_convert_type(w << 16, F32), lax.bitcast_convert_type(w, F32)


def _pack_rows(tabs, layer=0):
    if tabs.ndim == 2:
        tabs = tabs[None]
    _, n, d = tabs.shape
    rows = min(n, PACK_TILE)
    return pl.pallas_call(
        _pack_body, grid=(n // rows,),
        in_specs=[pl.BlockSpec((1, rows, d), lambda i: (layer, i, 0))],
        out_specs=pl.BlockSpec((rows, d // 2), lambda i: (i, 0)),
        out_shape=jax.ShapeDtypeStruct((n, d // 2), jnp.int32),
        compiler_params=_params("arbitrary"), name="pack_rows",
    )(tabs)


def _sc_group_loop(n_groups, stage, work):
    for cp in stage(0, 0):
        cp.start()

    def group(g, carry):
        slot = g % 2
        for cp in stage(g, slot):
            cp.wait()

        @pl.when(g + 1 < n_groups)
        def _():
            for cp in stage(g + 1, 1 - slot):
                cp.start()

        work(g, slot)
        return carry

    lax.fori_loop(0, n_groups, group, 0)


def _sc_udot_body(groups_per_worker, u_hbm, idx_hbm, x_hbm, a_hbm, idx_v, x_v, a_v, buf, tr, sems, ssems):
    g0 = _sc_worker_id() * groups_per_worker
    lanes = lax.iota(jnp.int32, SC_LANES)
    half = x_v.shape[2] // 2
    chunks = half // SC_LANES

    def stage(g, slot):
        gi = g0 + g
        return [pltpu.make_async_copy(idx_hbm.at[gi], idx_v.at[slot], ssems.at[slot, 0]),
                pltpu.make_async_copy(x_hbm.at[pl.ds(gi * SC_GROUP, SC_GROUP)], x_v.at[slot], ssems.at[slot, 1])]

    def work(g, slot):
        def compute(tok, row0, rows):
            for eb in range(SC_ROWS // SC_LANES):
                def body(c, accs):
                    x_lo = x_v[slot, tok, pl.ds(c * SC_LANES, SC_LANES)]
                    x_hi = x_v[slot, tok, pl.ds(half + c * SC_LANES, SC_LANES)]
                    out = []
                    for e in range(SC_LANES):
                        lo, hi = _unpack_words(rows[eb * SC_LANES + e, pl.ds(c * SC_LANES, SC_LANES)])
                        out.append(accs[e] + lo * x_lo + hi * x_hi)
                    return tuple(out)

                accs = lax.fori_loop(0, chunks, body, tuple(jnp.zeros((SC_LANES,), F32) for _ in range(SC_LANES)))
                for e in range(SC_LANES):
                    tr[e, :] = accs[e]
                tot = jnp.zeros((SC_LANES,), F32)
                for c in range(SC_LANES):
                    tot = tot + plsc.load_gather(tr, [lanes, jnp.full((SC_LANES,), c, jnp.int32)])
                a_v[tok, pl.ds(row0 + eb * SC_LANES, SC_LANES)] = tot

        _sc_gather_loop(u_hbm, idx_v.at[slot], buf, sems, compute)
        pltpu.sync_copy(a_v, a_hbm.at[g0 + g])

    _sc_group_loop(groups_per_worker, stage, work)


def _sc_vsum_body(groups_per_worker, v_hbm, idx_hbm, w_hbm, o_hbm, idx_v, w_v, o_v, buf, sems, ssems):
    g0 = _sc_worker_id() * groups_per_worker
    d = o_v.shape[1]
    words_per_pass = SC_LANES // 2
    n_passes = d // (2 * SC_LANES * words_per_pass)

    def stage(g, slot):
        gi = g0 + g
        return [pltpu.make_async_copy(idx_hbm.at[gi], idx_v.at[slot], ssems.at[slot, 0]),
                pltpu.make_async_copy(w_hbm.at[gi], w_v.at[slot], ssems.at[slot, 1])]

    def work(g, slot):
        slot_v = jnp.full((SC_LANES,), slot, jnp.int32)

        def compute(tok, row0, rows):
            tok_v = jnp.full((SC_LANES,), tok, jnp.int32)
            for dg in range(n_passes):
                def body(r, accs):
                    wk = plsc.load_gather(w_v, [slot_v, tok_v, jnp.full((SC_LANES,), row0 + r, jnp.int32)])
                    out = []
                    for cc in range(words_per_pass):
                        lo, hi = _unpack_words(rows[r, pl.ds((dg * words_per_pass + cc) * SC_LANES, SC_LANES)])
                        out += [accs[2 * cc] + wk * lo, accs[2 * cc + 1] + wk * hi]
                    return tuple(out)

                accs = lax.fori_loop(0, SC_ROWS, body, tuple(jnp.zeros((SC_LANES,), F32) for _ in range(SC_LANES)))
                for cc in range(SC_LANES):
                    word0 = (dg * words_per_pass + cc // 2) * SC_LANES
                    sl = (tok, pl.ds((cc % 2) * (d // 2) + word0, SC_LANES))
                    o_v[sl] = o_v[sl] + accs[cc]

        def zero(i, c):
            o_v[i // (d // SC_LANES), pl.ds((i % (d // SC_LANES)) * SC_LANES, SC_LANES)] = jnp.zeros((SC_LANES,), F32)
            return c

        lax.fori_loop(0, SC_GROUP * (d // SC_LANES), zero, 0)
        _sc_gather_loop(v_hbm, idx_v.at[slot], buf, sems, compute)
        pltpu.sync_copy(o_v, o_hbm.at[pl.ds((g0 + g) * SC_GROUP, SC_GROUP)])

    _sc_group_loop(groups_per_worker, stage, work)


def _gate_body(a_ref, g_ref, w_ref):
    a = a_ref[...]
    w_ref[...] = g_ref[...] * (0.5 * a * (1.0 + lax.erf(a * (2.0 ** -0.5))))


def _peer_retrieve(xn, idx, gate, u_tab, v_tab):
    t, d = xn.shape
    hk = idx.shape[1]
    n_groups = t // SC_GROUP
    groups_per_worker = n_groups // (SC_CORES * SC_SUBCORES)
    idx3 = idx.reshape(n_groups, SC_GROUP, hk)
    row_buf = pltpu.VMEM((2, SC_ROWS, d // 2), jnp.int32)
    sems = pltpu.SemaphoreType.DMA((2,))
    stage_sems = pltpu.SemaphoreType.DMA((2, 2))
    cost = pl.CostEstimate(flops=2 * t * hk * d, transcendentals=0,
                           bytes_accessed=2 * t * hk * d + 4 * (t * d + 2 * t * hk))

    a3 = pl.kernel(
        functools.partial(_sc_udot_body, groups_per_worker),
        out_type=jax.ShapeDtypeStruct((n_groups, SC_GROUP, hk), F32),
        mesh=_sc_mesh(),
        scratch_types=[pltpu.VMEM((2, SC_GROUP, hk), jnp.int32), pltpu.VMEM((2, SC_GROUP, d), F32),
                       pltpu.VMEM((SC_GROUP, hk), F32), row_buf, pltpu.VMEM((SC_LANES, SC_LANES), F32), sems,
                       stage_sems],
        compiler_params=pltpu.CompilerParams(needs_layout_passes=False),
        cost_estimate=cost,
        name="peer_udot",
    )(u_tab, idx3, xn)

    def finish():
        g3 = gate.reshape(n_groups, SC_GROUP, hk)
        gb = min(n_groups, 256)
        blk = pl.BlockSpec((gb, SC_GROUP, hk), lambda i: (i, 0, 0))
        w3 = _ordered_call(
            _gate_body, grid=(n_groups // gb,), in_specs=[blk, blk], out_specs=blk,
            out_shape=jax.ShapeDtypeStruct(a3.shape, F32), compiler_params=_params("arbitrary"), name="peer_gate",
        )(a3, g3)
        return pl.kernel(
            functools.partial(_sc_vsum_body, groups_per_worker),
            out_type=jax.ShapeDtypeStruct((t, d), F32),
            mesh=_sc_mesh(),
            scratch_types=[pltpu.VMEM((2, SC_GROUP, hk), jnp.int32), pltpu.VMEM((2, SC_GROUP, hk), F32),
                           pltpu.VMEM((SC_GROUP, d), F32), row_buf, sems, stage_sems],
            compiler_params=pltpu.CompilerParams(needs_layout_passes=False),
            cost_estimate=cost,
            name="peer_vsum",
        )(v_tab, idx3, w3)

    return finish


def _pack_rows_tc(words):
    n, half = words.shape
    return words.reshape(n * (half // LANES), LANES)


def _tc_gather_rows(tab_ref, idx_ref, i, ubuf, n_rows, lines):
    for k in range(n_rows):
        line0 = pl.multiple_of(idx_ref[i, k], lines)
        ubuf[pl.ds(k, lines, stride=TC_ROW_STRIDE), :] = tab_ref[pl.ds(line0, lines), :]


def _tc_halves(ubuf, jb, n_rows):
    return _unpack_words(ubuf[jb * TC_ROW_STRIDE:jb * TC_ROW_STRIDE + n_rows, :])


def _tc_udot_body(tab_ref, idx_ref, x_ref, o_ref, ubuf_a, ubuf_b):
    tt, d = x_ref.shape
    n_rows = o_ref.shape[0]
    lines = d // 2 // LANES
    lane = lax.broadcasted_iota(jnp.int32, (n_rows, tt), 1)

    def group(g, out):
        for j in range(SUBLANES):
            i = g * SUBLANES + j
            ubuf = (ubuf_a, ubuf_b)[j % 2]
            _tc_gather_rows(tab_ref, idx_ref, i, ubuf, n_rows, lines)
            xrow = x_ref[pl.ds(i, 1), :]
            acc = jnp.zeros((n_rows, LANES), F32)
            for jb in range(lines):
                lo, hi = _tc_halves(ubuf, jb, n_rows)
                acc = (acc + lo * xrow[:, jb * LANES:(jb + 1) * LANES]
                       + hi * xrow[:, d // 2 + jb * LANES:d // 2 + (jb + 1) * LANES])
            out = jnp.where(lane == i, jnp.sum(acc, axis=1, keepdims=True), out)
        return out

    o_ref[...] = lax.fori_loop(0, tt // SUBLANES, group, jnp.zeros((n_rows, tt), F32))


def _tc_vsum_body(tab_ref, idx_ref, w_ref, o_ref, ubuf_a, ubuf_b):
    tt, d = o_ref.shape
    n_rows = w_ref.shape[0]
    lines = d // 2 // LANES
    lane = lax.broadcasted_iota(jnp.int32, (n_rows, tt), 1)

    def group(g, carry):
        rows = []
        for j in range(SUBLANES):
            i = g * SUBLANES + j
            ubuf = (ubuf_a, ubuf_b)[j % 2]
            _tc_gather_rows(tab_ref, idx_ref, i, ubuf, n_rows, lines)
            col = jnp.sum(jnp.where(lane == i, w_ref[...], 0.0), axis=1, keepdims=True)
            halves = [_tc_halves(ubuf, jb, n_rows) for jb in range(lines)]
            rows.append(jnp.concatenate(
                [jnp.sum(lo * col, axis=0, keepdims=True) for lo, _ in halves]
                + [jnp.sum(hi * col, axis=0, keepdims=True) for _, hi in halves], axis=1))
        o_ref[pl.ds(pl.multiple_of(g * SUBLANES, SUBLANES), SUBLANES), :] = jnp.concatenate(rows, axis=0)
        return carry

    lax.fori_loop(0, tt // SUBLANES, group, 0)


def _tc_table_specs(tab4, hk, tt):
    lines_total = tab4.shape[0]
    return [pl.BlockSpec((lines_total, LANES), lambda i: (0, 0), pipeline_mode=pl.Buffered(1)),
            pl.BlockSpec((tt, hk), lambda i: (i, 0), memory_space=pltpu.SMEM)]


def _tc_udot(tab4, idx, x):
    t, d = x.shape
    hk = idx.shape[1]
    tt = min(TC_RETRIEVE_TILE, t)
    return _ordered_call(
        _tc_udot_body, grid=(t // tt,),
        in_specs=_tc_table_specs(tab4, hk, tt) + [pl.BlockSpec((tt, d), lambda i: (i, 0))],
        out_specs=pl.BlockSpec((hk, tt), lambda i: (0, i)),
        out_shape=jax.ShapeDtypeStruct((hk, t), F32),
        scratch_shapes=[pltpu.VMEM((d // 2 // LANES * TC_ROW_STRIDE, LANES), jnp.int32)] * 2,
        compiler_params=_params("arbitrary"), name="tc_udot",
    )(tab4, idx, x)


def _tc_vsum(tab4, idx, w_t):
    hk, t = w_t.shape
    d = tab4.shape[1] * 2 * (tab4.shape[0] // PEER_NKEYS ** 2)
    tt = min(TC_RETRIEVE_TILE, t)
    return _ordered_call(
        _tc_vsum_body, grid=(t // tt,),
        in_specs=_tc_table_specs(tab4, hk, tt) + [pl.BlockSpec((hk, tt), lambda i: (0, i))],
        out_specs=pl.BlockSpec((tt, d), lambda i: (i, 0)),
        out_shape=jax.ShapeDtypeStruct((t, d), F32),
        scratch_shapes=[pltpu.VMEM((d // 2 // LANES * TC_ROW_STRIDE, LANES), jnp.int32)] * 2,
        compiler_params=_params("arbitrary"), name="tc_vsum",
    )(tab4, idx, w_t)


def _peer_retrieve_tc(xn, idx_t, gate_t, u_tab4, v_tab4):
    hk, t = idx_t.shape
    idx = idx_t.T * (u_tab4.shape[0] // PEER_NKEYS ** 2)
    a_t = _tc_udot(u_tab4, idx, xn)

    def finish():
        tt = min(TC_RETRIEVE_TILE, t)
        blk = pl.BlockSpec((hk, tt), lambda i: (0, i))
        w_t = _ordered_call(
            _gate_body, grid=(t // tt,), in_specs=[blk, blk], out_specs=blk,
            out_shape=jax.ShapeDtypeStruct((hk, t), F32), compiler_params=_params("arbitrary"), name="tc_gate",
        )(a_t, gate_t)
        return _tc_vsum(v_tab4, idx, w_t)

    return finish


def _peer_start(x, g, w_q, keys, tables, on_tc):
    bsz, s_len, d = x.shape
    xn, idx_t, gate_t = _peer_route(x.reshape(bsz * s_len, d), g, w_q, keys)
    if on_tc:
        fin = _peer_retrieve_tc(xn, idx_t, gate_t, tables[2], tables[3])
    else:
        fin = _peer_retrieve(xn, idx_t.T, gate_t.T, tables[0], tables[1])
    return lambda: fin().reshape(bsz, s_len, d)


def _kv_body(fox_w, x_ref, po_ref, g_ref, w_ref, bf_ref, k_ref, v_ref, c_ref, ct_ref, carry):
    x = x_ref[0] + po_ref[0]
    tm = x.shape[0]
    hs = _rms(x, g_ref[...]).astype(BF16)
    z = jnp.dot(hs, w_ref[...], preferred_element_type=F32)
    k_ref[0] = z[:, :fox_w].astype(BF16)
    v_ref[0] = z[:, fox_w:2 * fox_w].astype(BF16)
    zf = z[:, 2 * fox_w:] + bf_ref[...]
    logf = jnp.minimum(zf, 0.0) - jnp.log1p(jnp.exp(-jnp.abs(zf)))

    @pl.when(pl.program_id(1) == 0)
    def _():
        carry[...] = jnp.zeros(carry.shape, F32)

    r = lax.broadcasted_iota(jnp.int32, (tm, tm), 0)
    c = lax.broadcasted_iota(jnp.int32, (tm, tm), 1)
    tri = (c <= r).astype(F32)
    cs = jnp.dot(tri, logf, preferred_element_type=F32, precision=lax.Precision.HIGHEST) + carry[...]
    carry[...] = cs[tm - 1:tm, :]
    c_ref[0] = cs
    ct_ref[0] = cs.T[0:ct_ref.shape[1], :]


def _kv_proj(x, po, g, w_kv, b_f):
    bsz, s_len, d = x.shape
    tm = min(TOKEN_TILE, s_len)
    n_heads = b_f.shape[0]
    fox_w = (w_kv.shape[1] - n_heads) // 2
    w = jnp.pad(w_kv, ((0, 0), (0, LANES - n_heads))).astype(BF16)
    bf = jnp.pad(b_f, (0, LANES - n_heads)).reshape(1, LANES)
    hp = -(-n_heads // SUBLANES) * SUBLANES
    tok = lambda w_, dt: (pl.BlockSpec((1, tm, w_), lambda b, i: (b, i, 0)), jax.ShapeDtypeStruct((bsz, s_len, w_), dt))
    (ks, ksh), (vs, vsh), (cs, csh) = tok(fox_w, BF16), tok(fox_w, BF16), tok(LANES, F32)
    return _ordered_call(
        functools.partial(_kv_body, fox_w),
        grid=(bsz, s_len // tm),
        in_specs=[
            pl.BlockSpec((1, tm, d), lambda b, i: (b, i, 0)),
            pl.BlockSpec((1, tm, d), lambda b, i: (b, i, 0)),
            pl.BlockSpec((1, d), lambda b, i: (0, 0)),
            pl.BlockSpec(w.shape, lambda b, i: (0, 0)),
            pl.BlockSpec((1, LANES), lambda b, i: (0, 0)),
        ],
        out_specs=[ks, vs, cs, pl.BlockSpec((1, hp, tm), lambda b, i: (b, 0, i))],
        out_shape=[ksh, vsh, csh, jax.ShapeDtypeStruct((bsz, hp, s_len), F32)],
        scratch_shapes=[pltpu.VMEM((1, LANES), F32)],
        compiler_params=_params("arbitrary", "arbitrary"),
        name="kv_proj",
    )(x, po, g.reshape(1, d), w, bf)


def _mix_b_in_body(fox_w, n_mem_heads, n_mem, x_ref, po_ref, g_ref, win_ref, mk_ref, mv_ref, q_ref, ym_ref):
    x = x_ref[0] + po_ref[0]
    h = _rms(x, g_ref[...]).astype(BF16)
    z = jnp.dot(h, win_ref[...], preferred_element_type=F32)
    q_ref[0] = (z[:, :fox_w] * (HEAD_DIM ** -0.5)).astype(BF16)
    ym_ref[0] = _mem_attend(z[:, fox_w:], mk_ref[0], mv_ref[0], n_mem_heads, n_mem).astype(BF16)


def _mix_b_in(x, po, g, w_in, mkbd, mvbd):
    bsz, s_len, d = x.shape
    tm = min(TOKEN_TILE, s_len)
    mem_w = mvbd.shape[2]
    fox_w = w_in.shape[1] - mem_w
    n_mem_heads = mem_w // HEAD_DIM
    n_mem = mkbd.shape[2] // n_mem_heads
    win = w_in.astype(BF16)
    tok = lambda w_: pl.BlockSpec((1, tm, w_), lambda b, i: (b, i, 0))
    per_b = lambda a: pl.BlockSpec((1,) + a.shape[1:], lambda b, i: (b, 0, 0))
    return _ordered_call(
        functools.partial(_mix_b_in_body, fox_w, n_mem_heads, n_mem),
        grid=(bsz, s_len // tm),
        in_specs=[tok(d), tok(d), pl.BlockSpec((1, d), lambda b, i: (0, 0)),
                  pl.BlockSpec(win.shape, lambda b, i: (0, 0)), per_b(mkbd), per_b(mvbd)],
        out_specs=[tok(fox_w), tok(mem_w)],
        out_shape=[jax.ShapeDtypeStruct((bsz, s_len, fox_w), BF16),
                   jax.ShapeDtypeStruct((bsz, s_len, mem_w), BF16)],
        compiler_params=_params("arbitrary", "arbitrary"),
        name="mix_b_in",
    )(x, po, g.reshape(1, d), win, mkbd, mvbd)


def _fox_body(n_heads, q_ref, k_ref, v_ref, c_ref, ct_ref, o_ref, m_scr, l_scr, cq_scr, acc):
    qi, ki = pl.program_id(1), pl.program_id(2)
    tq, tk = q_ref.shape[1], k_ref.shape[1]

    def wide(a):
        return jnp.concatenate([a] * (tk // LANES), axis=1)

    @pl.when(ki == 0)
    def _():
        m_scr[...] = jnp.full(m_scr.shape, -jnp.inf, F32)
        l_scr[...] = jnp.zeros(l_scr.shape, F32)
        acc[...] = jnp.zeros(acc.shape, F32)
        cq = c_ref[0]
        for h in range(n_heads):
            cq_scr[h] = jnp.broadcast_to(cq[:, h:h + 1], (tq, LANES))

    def step(masked):
        if masked:
            keep = (lax.broadcasted_iota(jnp.int32, (tq, tk), 0)
                    >= lax.broadcasted_iota(jnp.int32, (tq, tk), 1))
        for h in range(n_heads):
            sl = slice(h * HEAD_DIM, (h + 1) * HEAD_DIM)
            s = lax.dot_general(q_ref[0, :, sl], k_ref[0, :, sl], (((1,), (1,)), ((), ())),
                                preferred_element_type=F32)
            s = s + (wide(cq_scr[h]) - ct_ref[0, h:h + 1, :])
            if masked:
                s = jnp.where(keep, s, -jnp.inf)
            m_old = m_scr[h]
            m_new = jnp.maximum(m_old, jnp.max(s, axis=-1, keepdims=True))
            alpha = jnp.exp(m_old - m_new)
            p = jnp.exp(s - wide(m_new))
            l_scr[h] = alpha * l_scr[h] + jnp.sum(p, axis=-1, keepdims=True)
            acc[h] = alpha[:, :HEAD_DIM] * acc[h] + jnp.dot(p.astype(BF16), v_ref[0, :, sl],
                                                           preferred_element_type=F32)
            m_scr[h] = m_new

    @pl.when(ki < qi)
    def _():
        step(False)

    @pl.when(ki == qi)
    def _():
        step(True)
        for h in range(n_heads):
            sl = slice(h * HEAD_DIM, (h + 1) * HEAD_DIM)
            o_ref[0, :, sl] = (acc[h] / l_scr[h][:, :HEAD_DIM]).astype(BF16)


def _fox(q, k, v, c, ct):
    bsz, s_len, w = q.shape
    n_heads = w // HEAD_DIM
    t = min(ATTN_TILE, s_len)
    nb = s_len // t
    return _ordered_call(
        functools.partial(_fox_body, n_heads),
        grid=(bsz, nb, nb),
        in_specs=[
            pl.BlockSpec((1, t, w), lambda b, i, j: (b, i, 0)),
            pl.BlockSpec((1, t, w), lambda b, i, j: (b, jnp.minimum(i, j), 0)),
            pl.BlockSpec((1, t, w), lambda b, i, j: (b, jnp.minimum(i, j), 0)),
            pl.BlockSpec((1, t, c.shape[2]), lambda b, i, j: (b, i, 0)),
            pl.BlockSpec((1, ct.shape[1], t), lambda b, i, j: (b, 0, jnp.minimum(i, j))),
        ],
        out_specs=pl.BlockSpec((1, t, w), lambda b, i, j: (b, i, 0)),
        out_shape=jax.ShapeDtypeStruct((bsz, s_len, w), BF16),
        scratch_shapes=[pltpu.VMEM((n_heads, t, LANES), F32), pltpu.VMEM((n_heads, t, LANES), F32),
                        pltpu.VMEM((n_heads, t, LANES), F32), pltpu.VMEM((n_heads, t, HEAD_DIM), F32)],
        compiler_params=_params("arbitrary", "arbitrary", "arbitrary"),
        name="fox",
    )(q, k, v, c, ct)


def _mix_b_out_body(fox_w, x_ref, po_ref, y_ref, ym_ref, wout_ref, o_ref):
    o_ref[0] = (x_ref[0] + po_ref[0]
                + jnp.dot(y_ref[0], wout_ref[0:fox_w, :], preferred_element_type=F32)
                + jnp.dot(ym_ref[0], wout_ref[fox_w:, :], preferred_element_type=F32))


def _mix_b_out(x, po, y, ym, w_out):
    bsz, s_len, d = x.shape
    tm = min(TOKEN_TILE, s_len)
    fox_w = y.shape[2]
    wout = w_out.astype(BF16)
    tok = lambda w_: pl.BlockSpec((1, tm, w_), lambda b, i: (b, i, 0))
    return _ordered_call(
        functools.partial(_mix_b_out_body, fox_w),
        grid=(bsz, s_len // tm),
        in_specs=[tok(d), tok(d), tok(fox_w), tok(ym.shape[2]), pl.BlockSpec(wout.shape, lambda b, i: (0, 0))],
        out_specs=tok(d),
        out_shape=jax.ShapeDtypeStruct(x.shape, F32),
        compiler_params=_params("arbitrary", "arbitrary"),
        name="mix_b_out",
    )(x, po, y, ym, wout)


def _final_body(x_ref, po_ref, g_ref, o_ref):
    o_ref[0] = _rms(x_ref[0] + po_ref[0], g_ref[...])


def _final_norm(x, po, g):
    bsz, s_len, d = x.shape
    tm = min(TOKEN_TILE, s_len)
    tok = pl.BlockSpec((1, tm, d), lambda b, i: (b, i, 0))
    return _ordered_call(
        _final_body,
        grid=(bsz, s_len // tm),
        in_specs=[tok, tok, pl.BlockSpec((1, d), lambda b, i: (0, 0))],
        out_specs=tok,
        out_shape=jax.ShapeDtypeStruct(x.shape, F32),
        compiler_params=_params("arbitrary", "arbitrary"),
        name="final_norm",
    )(x, po, g.reshape(1, d))


def kernel(x, mem, g_mix, w_in_a, conv_w, w_in_b, w_out, g_mem, w_mem_kv, g_kv, w_kv, b_f, g_ffn,
           peer_wq, peer_keys, peer_u, peer_v, g_final):
    depth = g_mix.shape[0]
    n_a = w_in_a.shape[0]
    n_mem_heads = w_mem_kv.shape[2] // (2 * HEAD_DIM)
    mkbd, mvbd = _block_diag_mem(_memkv(mem, g_mem, w_mem_kv), n_mem_heads)

    bsz = x.shape[0]
    eighths = sum(r for r, _ in BATCH_STREAMS)
    plan, unit = (BATCH_STREAMS, bsz // eighths) if bsz % eighths == 0 else (((1, "s"),), bsz)
    tables = []
    for l in range(depth):
        u_pk, v_pk = _pack_rows(peer_u, l), _pack_rows(peer_v, l)
        tables.append((u_pk, v_pk, _pack_rows_tc(u_pk), _pack_rows_tc(v_pk)))
    states, lo = [], 0
    for r, letters in plan:
        rows = r * unit
        states.append(dict(x=x[lo:lo + rows], po=None, lo=lo, hi=lo + rows,
                           units=[letters[min(l, len(letters) - 1)] == "t" for l in range(depth + 1)]))
        lo += rows
    _issue.last = None

    def front(st, layer):
        xs, po = st["x"], st["po"]
        mk, mv = mkbd[layer, st["lo"]:st["hi"]], mvbd[layer, st["lo"]:st["hi"]]
        if layer == n_a:
            st["kv"] = _kv_proj(xs, po, g_kv, w_kv, b_f)
        if layer < n_a and layer == 0:
            xs = _mix_a(x, None, g_mix[0], w_in_a[0], conv_w[0], mk, mv, w_out[0], row0=st["lo"])
        elif layer < n_a:
            xs = _mix_a(xs, po, g_mix[layer], w_in_a[layer], conv_w[layer], mk, mv, w_out[layer])
        else:
            q, ym = _mix_b_in(xs, po, g_mix[layer], w_in_b[layer - n_a], mk, mv)
            y = _fox(q, *st["kv"])
            xs = _mix_b_out(xs, po, y, ym, w_out[layer])
        st["x"] = xs
        st["fin"] = _peer_start(xs, g_ffn[layer], peer_wq[layer], peer_keys[layer], tables[layer],
                                st["units"][layer])

    def back(st):
        st["po"] = st["fin"]()

    for st in states:
        st.update(layer=0, next="front", ready=0.0)
    tc_clock = sc_clock = 0.0
    while any(st["layer"] < depth for st in states):
        live = [st for st in states if st["layer"] < depth]
        st = min(live, key=lambda s: (
            max(s["ready"], tc_clock),
            not (s["next"] == "back" and not s["on_tc"]),
            s["on_tc"] if s["next"] == "back" else s["units"][s["layer"]]))
        layer, rows = st["layer"], (st["hi"] - st["lo"]) / unit
        tc_clock = max(tc_clock, st["ready"])
        if st["next"] == "front":
            st["on_tc"] = st["units"][layer]
            front(st, layer)
            tc_clock += rows * (COST_ROUTE + (COST_MIX_ATTN if layer >= n_a else COST_MIX_CONV)
                                + (COST_TC_UDOT if st["on_tc"] else 0.0))
            st["next"] = "back"
            if st["on_tc"]:
                st["ready"] = tc_clock
            else:
                sc_clock = max(sc_clock, tc_clock) + rows * COST_SC_PASS
                st["ready"] = sc_clock
        else:
            back(st)
            if st["on_tc"]:
                tc_clock += rows * COST_TC_BACK
                st["ready"] = tc_clock
            else:
                sc_clock = max(sc_clock, tc_clock) + rows * COST_SC_PASS
                st["ready"] = sc_clock
            st["next"], st["layer"] = "front", layer + 1
    outs = [_final_norm(st["x"], st["po"], g_final) for st in states]
    _issue.last = None
    return jnp.concatenate(outs, axis=0)
```

```python
import functools

import jax
import jax.numpy as jnp
from jax import lax
from jax.experimental import pallas as pl
from jax.experimental.pallas import tpu as pltpu
from jax.experimental.pallas import tpu_sc as plsc

EPS = 1e-6
HEAD_DIM = 64
CONV_K = 3
PEER_HEADS = 8
PEER_NKEYS = 128
PEER_HALF = 128
PEER_TOPK = 16
PEER_CHUNK = 128
LANES = 128
SUBLANES = 8
TOKEN_TILE = 512
ROUTE_TILE = 256
ROUTE_HEADS = 8
ATTN_TILE = 512
VMEM_LIMIT = 56 * 1024 * 1024
SC_CORES = 2
SC_SUBCORES = 16
SC_LANES = 16
SC_ROWS = 64
SC_GROUP = 16
PACK_TILE = 1024
TC_RETRIEVE_TILE = 128
TC_ROW_STRIDE = 136
COST_MIX_CONV, COST_MIX_ATTN, COST_ROUTE = 0.06, 0.36, 0.22
COST_SC_PASS, COST_TC_UDOT, COST_TC_BACK = 0.70, 0.63, 0.57
BATCH_STREAMS = ((1, "ssss"), (2, "ssss"), (2, "ssss"), (1, "tttt"), (1, "tttt"), (1, "tttt"))

BF16 = jnp.bfloat16
F32 = jnp.float32


def _params(*sem):
    return pltpu.CompilerParams(dimension_semantics=sem, vmem_limit_bytes=VMEM_LIMIT)


class _IssueOrder:
    last = None


_issue = _IssueOrder()


def _ordered_call(body, *, in_specs, **kw):
    n_in = len(in_specs)

    def run(*args):
        prev = _issue.last
        if prev is None:
            outs = pl.pallas_call(body, in_specs=in_specs, **kw)(*args)
        else:
            def with_token(*refs):
                body(*refs[:n_in], *refs[n_in + 1:])

            outs = pl.pallas_call(with_token, in_specs=list(in_specs) + [pl.BlockSpec(memory_space=pl.ANY)],
                                  **kw)(*args, prev)
        _issue.last = jax.tree.leaves(outs)[0]
        return outs

    return run


def _rms(x, g):
    return x * lax.rsqrt(jnp.mean(x * x, axis=-1, keepdims=True) + EPS) * g


def _mem_attend(qm, mkbd, mvbd, n_heads, n_mem):
    lg = jnp.dot(qm.astype(BF16), mkbd, preferred_element_type=F32) * (HEAD_DIM ** -0.5)
    parts = []
    for h in range(n_heads):
        l = lg[:, h * n_mem:(h + 1) * n_mem]
        e = jnp.exp(l - jnp.max(l, axis=-1, keepdims=True))
        parts.append((e / jnp.sum(e, axis=-1, keepdims=True)).astype(BF16))
    return jnp.dot(jnp.concatenate(parts, axis=1), mvbd, preferred_element_type=F32)


def _memkv_body(mem_ref, g_ref, w_ref, o_ref):
    mn = _rms(mem_ref[0], g_ref[0])
    o_ref[0, 0] = jnp.dot(mn.astype(BF16), w_ref[0], preferred_element_type=F32).astype(BF16)


def _memkv(mem, g_mem, w_mem_kv):
    bsz, n_mem, d = mem.shape
    depth, _, wd = w_mem_kv.shape
    return _ordered_call(
        _memkv_body,
        grid=(depth, bsz),
        in_specs=[
            pl.BlockSpec((1, n_mem, d), lambda l, b: (b, 0, 0)),
            pl.BlockSpec((1, 1, d), lambda l, b: (l, 0, 0)),
            pl.BlockSpec((1, d, wd), lambda l, b: (l, 0, 0)),
        ],
        out_specs=pl.BlockSpec((1, 1, n_mem, wd), lambda l, b: (l, b, 0, 0)),
        out_shape=jax.ShapeDtypeStruct((depth, bsz, n_mem, wd), BF16),
        compiler_params=_params("arbitrary", "arbitrary"),
        name="memkv",
    )(mem, g_mem.reshape(depth, 1, d), w_mem_kv.astype(BF16))


def _block_diag_mem(mkv, n_heads):
    depth, bsz, n_mem, _ = mkv.shape
    r = mkv.reshape(depth, bsz, n_mem, 2, n_heads, HEAD_DIM)
    eye = jnp.eye(n_heads, dtype=mkv.dtype)
    mk = jnp.einsum("lbmhd,hg->lbhdgm", r[:, :, :, 0], eye)
    mv = jnp.einsum("lbmhd,hg->lbhmgd", r[:, :, :, 1], eye)
    hw = n_heads * HEAD_DIM
    return mk.reshape(depth, bsz, hw, n_heads * n_mem), mv.reshape(depth, bsz, n_heads * n_mem, hw)


def _mix_a_body(has_po, conv_ch, n_mem_heads, n_mem, *refs):
    if has_po:
        x_ref, po_ref, g_ref, win_ref, cw_ref, mk_ref, mv_ref, wout_ref, o_ref, cbuf = refs
        x = x_ref[0] + po_ref[0]
    else:
        x_ref, g_ref, win_ref, cw_ref, mk_ref, mv_ref, wout_ref, o_ref, cbuf = refs
        x = x_ref[0]
    tm = x.shape[0]
    h = _rms(x, g_ref[...]).astype(BF16)
    z = jnp.dot(h, win_ref[...], preferred_element_type=F32)
    u = z[:, :conv_ch]
    c_gate = z[:, conv_ch:2 * conv_ch]
    b_gate = z[:, 2 * conv_ch:3 * conv_ch]
    qm = z[:, 3 * conv_ch:]
    cu = c_gate * u

    @pl.when(pl.program_id(1) == 0)
    def _():
        cbuf[0:SUBLANES, :] = jnp.zeros((SUBLANES, conv_ch), F32)

    cbuf[SUBLANES:SUBLANES + tm, :] = cu
    cu1 = cbuf[SUBLANES - 1:SUBLANES - 1 + tm, :]
    cu2 = cbuf[SUBLANES - 2:SUBLANES - 2 + tm, :]
    cw = cw_ref[...]
    y = b_gate * (cw[0:1] * cu2 + cw[1:2] * cu1 + cw[2:3] * cu)
    cbuf[0:SUBLANES, :] = cbuf[tm:tm + SUBLANES, :]

    ymem = _mem_attend(qm, mk_ref[0], mv_ref[0], n_mem_heads, n_mem)
    o_ref[0] = (x
                + jnp.dot(y.astype(BF16), wout_ref[0:conv_ch, :], preferred_element_type=F32)
                + jnp.dot(ymem.astype(BF16), wout_ref[conv_ch:, :], preferred_element_type=F32))


def _mix_a(x, po, g, w_in, conv_w, mkbd, mvbd, w_out, row0=0):
    bsz, s_len, d = mkbd.shape[0], x.shape[1], x.shape[2]
    tm = min(TOKEN_TILE, s_len)
    conv_ch = conv_w.shape[1]
    mem_w = w_in.shape[1] - 3 * conv_ch
    n_mem_heads = mem_w // HEAD_DIM
    n_mem = mkbd.shape[2] // n_mem_heads
    tok = pl.BlockSpec((1, tm, d), lambda b, i: (b, i, 0))
    full = lambda a: pl.BlockSpec(a.shape, lambda b, i: (0,) * a.ndim)
    per_b = lambda a: pl.BlockSpec((1,) + a.shape[1:], lambda b, i: (b, 0, 0))
    g2 = g.reshape(1, d)
    win = w_in.astype(BF16)
    wout = w_out.astype(BF16)
    args = [x] + ([po] if po is not None else []) + [g2, win, conv_w, mkbd, mvbd, wout]
    tok_x = pl.BlockSpec((1, tm, d), lambda b, i: (b + row0, i, 0))
    specs = [tok_x] + ([tok] if po is not None else []) + [full(g2), full(win), full(conv_w), per_b(mkbd), per_b(mvbd), full(wout)]
    return _ordered_call(
        functools.partial(_mix_a_body, po is not None, conv_ch, n_mem_heads, n_mem),
        grid=(bsz, s_len // tm),
        in_specs=specs,
        out_specs=tok,
        out_shape=jax.ShapeDtypeStruct((bsz, s_len, d), F32),
        scratch_shapes=[pltpu.VMEM((tm + SUBLANES, conv_ch), F32)],
        compiler_params=_params("arbitrary", "arbitrary"),
        name="mix_a",
    )(*args)


def _topk_rows(problems, k):
    state = []
    for s, payload in problems:
        n, cols = s.shape
        sub = lax.broadcasted_iota(jnp.int32, (SUBLANES, cols), 0)
        tiles = range(n // SUBLANES)
        state.append(dict(
            n=n, vals=[s[i * SUBLANES:(i + 1) * SUBLANES] for i in tiles], rows=[sub + i * SUBLANES for i in tiles],
            pays=None if payload is None else [payload[i * SUBLANES:(i + 1) * SUBLANES] for i in tiles],
            out_v=[], out_i=[]))
    for _ in range(k):
        for p in state:
            items = [(v, r) + (() if p["pays"] is None else (q,))
                     for v, r, q in zip(p["vals"], p["rows"], p["pays"] or p["rows"])]
            while len(items) > 1:
                nxt = []
                for a, b in zip(items[0::2], items[1::2]):
                    keep_a = a[0] >= b[0]
                    nxt.append(tuple(jnp.where(keep_a, x, y) for x, y in zip(a, b)))
                if len(items) % 2:
                    nxt.append(items[-1])
                items = nxt
            p["best"] = items[0]
        for p in state:
            best = p["best"]
            m = jnp.max(best[0], axis=0, keepdims=True)
            pos = jnp.min(jnp.where(best[0] == m, best[1], p["n"]), axis=0, keepdims=True)
            p["out_v"].append(m)
            if p["pays"] is None:
                p["out_i"].append(pos)
            else:
                p["out_i"].append(jnp.sum(jnp.where(best[1] == pos, best[2], 0), axis=0, keepdims=True))
            p["vals"] = [jnp.where(r == pos, -jnp.inf, v) for v, r in zip(p["vals"], p["rows"])]
    return [(jnp.concatenate(p["out_v"], axis=0), jnp.concatenate(p["out_i"], axis=0)) for p in state]


def _route_body(x_ref, g_ref, wq_ref, keys_ref, xn_ref, idx_ref, gate_ref, xnb):
    @pl.when(pl.program_id(1) == 0)
    def _():
        xn = _rms(x_ref[...], g_ref[...])
        xn_ref[...] = xn
        xnb[...] = xn.astype(BF16)

    n_heads = keys_ref.shape[0]
    q = jnp.dot(xnb[...], wq_ref[...], preferred_element_type=F32)
    scores = []
    for hp in range(2 * n_heads):
        qp = q[:, hp * PEER_HALF:(hp + 1) * PEER_HALF].astype(BF16)
        scores.append((lax.dot_general(keys_ref[hp // 2, hp % 2], qp, (((1,), (1,)), ((), ())),
                                       preferred_element_type=F32), None))
    tops = _topk_rows(scores, PEER_TOPK)
    cands = []
    for h in range(n_heads):
        (v1, i1), (v2, i2) = tops[2 * h], tops[2 * h + 1]
        cand, cidx, i = [], [], 0
        while i < PEER_TOPK:
            n_j = PEER_TOPK // (i + 1)
            if n_j > 1:
                n_j = min(PEER_TOPK, -(-n_j // SUBLANES) * SUBLANES)
                cand.append(v1[i:i + 1] + v2[0:n_j])
                cidx.append(i1[i:i + 1] * PEER_NKEYS + i2[0:n_j])
                i += 1
            else:
                cand.append(v1[i:] + v2[0:1])
                cidx.append(i1[i:] * PEER_NKEYS + i2[0:1])
                i = PEER_TOPK
        cands.append((jnp.concatenate(cand, axis=0), jnp.concatenate(cidx, axis=0)))
    for h, (top_s, idx) in enumerate(_topk_rows(cands, PEER_TOPK)):
        e = jnp.exp(top_s - top_s[0:1])
        idx_ref[h * PEER_TOPK:(h + 1) * PEER_TOPK, :] = idx
        gate_ref[h * PEER_TOPK:(h + 1) * PEER_TOPK, :] = e / jnp.sum(e, axis=0, keepdims=True)


def _peer_route(x2, g, w_q, keys):
    t, d = x2.shape
    tm = min(ROUTE_TILE, t)
    hs = ROUTE_HEADS
    dk = 2 * PEER_HALF * hs
    return _ordered_call(
        _route_body,
        grid=(t // tm, PEER_HEADS // hs),
        in_specs=[
            pl.BlockSpec((tm, d), lambda i, h: (i, 0)),
            pl.BlockSpec((1, d), lambda i, h: (0, 0)),
            pl.BlockSpec((d, dk), lambda i, h: (0, h)),
            pl.BlockSpec((hs, 2, PEER_NKEYS, PEER_HALF), lambda i, h: (h, 0, 0, 0)),
        ],
        out_specs=[
            pl.BlockSpec((tm, d), lambda i, h: (i, 0)),
            pl.BlockSpec((hs * PEER_TOPK, tm), lambda i, h: (h, i)),
            pl.BlockSpec((hs * PEER_TOPK, tm), lambda i, h: (h, i)),
        ],
        out_shape=[
            jax.ShapeDtypeStruct((t, d), F32),
            jax.ShapeDtypeStruct((PEER_HEADS * PEER_TOPK, t), jnp.int32),
            jax.ShapeDtypeStruct((PEER_HEADS * PEER_TOPK, t), F32),
        ],
        scratch_shapes=[pltpu.VMEM((tm, d), BF16)],
        compiler_params=_params("arbitrary", "arbitrary"),
        name="peer_route",
    )(x2, g.reshape(1, d), w_q.astype(BF16), keys.astype(BF16))


def _sc_mesh():
    return plsc.VectorSubcoreMesh(core_axis_name="c", subcore_axis_name="s",
                                  num_cores=SC_CORES, num_subcores=SC_SUBCORES)


def _sc_worker_id():
    return lax.axis_index("s") * SC_CORES + lax.axis_index("c")


def _sc_gather_loop(tab_hbm, idx_v, buf, sems, compute):
    per_tok = idx_v.shape[1] // SC_ROWS
    n_batches = SC_GROUP * per_tok

    def gather(j, slot):
        tok = j // per_tok
        row0 = pl.multiple_of((j % per_tok) * SC_ROWS, SC_ROWS)
        return tok, row0, pltpu.make_async_copy(
            tab_hbm.at[idx_v.at[tok, pl.ds(row0, SC_ROWS)]], buf.at[slot], sems.at[slot])

    nbuf = buf.shape[0]
    for b in range(nbuf - 1):
        gather(b, b)[2].start()

    def ring(jj, carry):
        for b in range(nbuf):
            j = jj * nbuf + b
            ahead = j + nbuf - 1

            @pl.when(ahead < n_batches)
            def _():
                gather(ahead, (b + nbuf - 1) % nbuf)[2].start()

            tok, row0, cp = gather(j, b)
            cp.wait()
            compute(tok, row0, buf.at[b])
        return carry

    lax.fori_loop(0, n_batches // nbuf, ring, 0)


def _pack_body(t3_ref, o_ref):
    _pack_tile(t3_ref.at[0], o_ref)


def _pack_tile(t_ref, o_ref):
    x = t_ref[...]
    half = x.shape[1] // 2
    lo = lax.bitcast_convert_type(x[:, :half], jnp.int32)
    hi = lax.bitcast_convert_type(x[:, half:], jnp.int32)
    srl = lax.shift_right_logical
    lo_bits = srl(lo + 0x7FFF + (srl(lo, 16) & 1), 16)
    mag = (hi & 0x7FFFFFFF) + 0x8000
    top = jnp.where(mag >= lo_bits, srl(mag - lo_bits, 16), 0)
    o_ref[...] = (hi & jnp.int32(-2 ** 31)) | (top << 16) | lo_bits


def _unpack_words(w):
    return lax.bitcast_convert_type(w << 16, F32), lax.bitcast_convert_type(w, F32)


def _pack_rows(tabs, layer=0):
    if tabs.ndim == 2:
        tabs = tabs[None]
    _, n, d = tabs.shape
    rows = min(n, PACK_TILE)
    return pl.pallas_call(
        _pack_body, grid=(n // rows,),
        in_specs=[pl.BlockSpec((1, rows, d), lambda i: (layer, i, 0))],
        out_specs=pl.BlockSpec((rows, d // 2), lambda i: (i, 0)),
        out_shape=jax.ShapeDtypeStruct((n, d // 2), jnp.int32),
        compiler_params=_params("arbitrary"), name="pack_rows",
    )(tabs)


def _sc_group_loop(n_groups, stage, work):
    for cp in stage(0, 0):
        cp.start()

    def group(g, carry):
        slot = g % 2
        for cp in stage(g, slot):
            cp.wait()

        @pl.when(g + 1 < n_groups)
        def _():
            for cp in stage(g + 1, 1 - slot):
                cp.start()

        work(g, slot)
        return carry

    lax.fori_loop(0, n_groups, group, 0)


def _sc_udot_body(groups_per_worker, u_hbm, idx_hbm, x_hbm, a_hbm, idx_v, x_v, a_v, buf, tr, sems, ssems):
    g0 = _sc_worker_id() * groups_per_worker
    lanes = lax.iota(jnp.int32, SC_LANES)
    half = x_v.shape[2] // 2
    chunks = half // SC_LANES

    def stage(g, slot):
        gi = g0 + g
        return [pltpu.make_async_copy(idx_hbm.at[gi], idx_v.at[slot], ssems.at[slot, 0]),
                pltpu.make_async_copy(x_hbm.at[pl.ds(gi * SC_GROUP, SC_GROUP)], x_v.at[slot], ssems.at[slot, 1])]

    def work(g, slot):
        def compute(tok, row0, rows):
            for eb in range(SC_ROWS // SC_LANES):
                def body(c, accs):
                    x_lo = x_v[slot, tok, pl.ds(c * SC_LANES, SC_LANES)]
                    x_hi = x_v[slot, tok, pl.ds(half + c * SC_LANES, SC_LANES)]
                    out = []
                    for e in range(SC_LANES):
                        lo, hi = _unpack_words(rows[eb * SC_LANES + e, pl.ds(c * SC_LANES, SC_LANES)])
                        out.append(accs[e] + lo * x_lo + hi * x_hi)
                    return tuple(out)

                accs = lax.fori_loop(0, chunks, body, tuple(jnp.zeros((SC_LANES,), F32) for _ in range(SC_LANES)))
                for e in range(SC_LANES):
                    tr[e, :] = accs[e]
                tot = jnp.zeros((SC_LANES,), F32)
                for c in range(SC_LANES):
                    tot = tot + plsc.load_gather(tr, [lanes, jnp.full((SC_LANES,), c, jnp.int32)])
                a_v[tok, pl.ds(row0 + eb * SC_LANES, SC_LANES)] = tot

        _sc_gather_loop(u_hbm, idx_v.at[slot], buf, sems, compute)
        pltpu.sync_copy(a_v, a_hbm.at[g0 + g])

    _sc_group_loop(groups_per_worker, stage, work)


def _sc_vsum_body(groups_per_worker, v_hbm, idx_hbm, w_hbm, o_hbm, idx_v, w_v, o_v, buf, sems, ssems):
    g0 = _sc_worker_id() * groups_per_worker
    d = o_v.shape[1]
    words_per_pass = SC_LANES // 2
    n_passes = d // (2 * SC_LANES * words_per_pass)

    def stage(g, slot):
        gi = g0 + g
        return [pltpu.make_async_copy(idx_hbm.at[gi], idx_v.at[slot], ssems.at[slot, 0]),
                pltpu.make_async_copy(w_hbm.at[gi], w_v.at[slot], ssems.at[slot, 1])]

    def work(g, slot):
        slot_v = jnp.full((SC_LANES,), slot, jnp.int32)

        def compute(tok, row0, rows):
            tok_v = jnp.full((SC_LANES,), tok, jnp.int32)
            for dg in range(n_passes):
                def body(r, accs):
                    wk = plsc.load_gather(w_v, [slot_v, tok_v, jnp.full((SC_LANES,), row0 + r, jnp.int32)])
                    out = []
                    for cc in range(words_per_pass):
                        lo, hi = _unpack_words(rows[r, pl.ds((dg * words_per_pass + cc) * SC_LANES, SC_LANES)])
                        out += [accs[2 * cc] + wk * lo, accs[2 * cc + 1] + wk * hi]
                    return tuple(out)

                accs = lax.fori_loop(0, SC_ROWS, body, tuple(jnp.zeros((SC_LANES,), F32) for _ in range(SC_LANES)))
                for cc in range(SC_LANES):
                    word0 = (dg * words_per_pass + cc // 2) * SC_LANES
                    sl = (tok, pl.ds((cc % 2) * (d // 2) + word0, SC_LANES))
                    o_v[sl] = o_v[sl] + accs[cc]

        def zero(i, c):
            o_v[i // (d // SC_LANES), pl.ds((i % (d // SC_LANES)) * SC_LANES, SC_LANES)] = jnp.zeros((SC_LANES,), F32)
            return c

        lax.fori_loop(0, SC_GROUP * (d // SC_LANES), zero, 0)
        _sc_gather_loop(v_hbm, idx_v.at[slot], buf, sems, compute)
        pltpu.sync_copy(o_v, o_hbm.at[pl.ds((g0 + g) * SC_GROUP, SC_GROUP)])

    _sc_group_loop(groups_per_worker, stage, work)


def _gate_body(a_ref, g_ref, w_ref):
    a = a_ref[...]
    w_ref[...] = g_ref[...] * (0.5 * a * (1.0 + lax.erf(a * (2.0 ** -0.5))))


def _peer_retrieve(xn, idx, gate, u_tab, v_tab):
    t, d = xn.shape
    hk = idx.shape[1]
    n_groups = t // SC_GROUP
    groups_per_worker = n_groups // (SC_CORES * SC_SUBCORES)
    idx3 = idx.reshape(n_groups, SC_GROUP, hk)
    row_buf = pltpu.VMEM((2, SC_ROWS, d // 2), jnp.int32)
    sems = pltpu.SemaphoreType.DMA((2,))
    stage_sems = pltpu.SemaphoreType.DMA((2, 2))
    cost = pl.CostEstimate(flops=2 * t * hk * d, transcendentals=0,
                           bytes_accessed=2 * t * hk * d + 4 * (t * d + 2 * t * hk))

    a3 = pl.kernel(
        functools.partial(_sc_udot_body, groups_per_worker),
        out_type=jax.ShapeDtypeStruct((n_groups, SC_GROUP, hk), F32),
        mesh=_sc_mesh(),
        scratch_types=[pltpu.VMEM((2, SC_GROUP, hk), jnp.int32), pltpu.VMEM((2, SC_GROUP, d), F32),
                       pltpu.VMEM((SC_GROUP, hk), F32), row_buf, pltpu.VMEM((SC_LANES, SC_LANES), F32), sems,
                       stage_sems],
        compiler_params=pltpu.CompilerParams(needs_layout_passes=False),
        cost_estimate=cost,
        name="peer_udot",
    )(u_tab, idx3, xn)

    def finish():
        g3 = gate.reshape(n_groups, SC_GROUP, hk)
        gb = min(n_groups, 256)
        blk = pl.BlockSpec((gb, SC_GROUP, hk), lambda i: (i, 0, 0))
        w3 = _ordered_call(
            _gate_body, grid=(n_groups // gb,), in_specs=[blk, blk], out_specs=blk,
            out_shape=jax.ShapeDtypeStruct(a3.shape, F32), compiler_params=_params("arbitrary"), name="peer_gate",
        )(a3, g3)
        return pl.kernel(
            functools.partial(_sc_vsum_body, groups_per_worker),
            out_type=jax.ShapeDtypeStruct((t, d), F32),
            mesh=_sc_mesh(),
            scratch_types=[pltpu.VMEM((2, SC_GROUP, hk), jnp.int32), pltpu.VMEM((2, SC_GROUP, hk), F32),
                           pltpu.VMEM((SC_GROUP, d), F32), row_buf, sems, stage_sems],
            compiler_params=pltpu.CompilerParams(needs_layout_passes=False),
            cost_estimate=cost,
            name="peer_vsum",
        )(v_tab, idx3, w3)

    return finish


def _pack_rows_tc(words):
    n, half = words.shape
    return words.reshape(n * (half // LANES), LANES)


def _tc_gather_rows(tab_ref, idx_ref, i, ubuf, n_rows, lines):
    for k in range(n_rows):
        line0 = pl.multiple_of(idx_ref[i, k], lines)
        ubuf[pl.ds(k, lines, stride=TC_ROW_STRIDE), :] = tab_ref[pl.ds(line0, lines), :]


def _tc_halves(ubuf, jb, n_rows):
    return _unpack_words(ubuf[jb * TC_ROW_STRIDE:jb * TC_ROW_STRIDE + n_rows, :])


def _tc_udot_body(tab_ref, idx_ref, x_ref, o_ref, ubuf_a, ubuf_b):
    tt, d = x_ref.shape
    n_rows = o_ref.shape[0]
    lines = d // 2 // LANES
    lane = lax.broadcasted_iota(jnp.int32, (n_rows, tt), 1)

    def group(g, out):
        for j in range(SUBLANES):
            i = g * SUBLANES + j
            ubuf = (ubuf_a, ubuf_b)[j % 2]
            _tc_gather_rows(tab_ref, idx_ref, i, ubuf, n_rows, lines)
            xrow = x_ref[pl.ds(i, 1), :]
            acc = jnp.zeros((n_rows, LANES), F32)
            for jb in range(lines):
                lo, hi = _tc_halves(ubuf, jb, n_rows)
                acc = (acc + lo * xrow[:, jb * LANES:(jb + 1) * LANES]
                       + hi * xrow[:, d // 2 + jb * LANES:d // 2 + (jb + 1) * LANES])
            out = jnp.where(lane == i, jnp.sum(acc, axis=1, keepdims=True), out)
        return out

    o_ref[...] = lax.fori_loop(0, tt // SUBLANES, group, jnp.zeros((n_rows, tt), F32))


def _tc_vsum_body(tab_ref, idx_ref, w_ref, o_ref, ubuf_a, ubuf_b):
    tt, d = o_ref.shape
    n_rows = w_ref.shape[0]
    lines = d // 2 // LANES
    lane = lax.broadcasted_iota(jnp.int32, (n_rows, tt), 1)

    def group(g, carry):
        rows = []
        for j in range(SUBLANES):
            i = g * SUBLANES + j
            ubuf = (ubuf_a, ubuf_b)[j % 2]
            _tc_gather_rows(tab_ref, idx_ref, i, ubuf, n_rows, lines)
            col = jnp.sum(jnp.where(lane == i, w_ref[...], 0.0), axis=1, keepdims=True)
            halves = [_tc_halves(ubuf, jb, n_rows) for jb in range(lines)]
            rows.append(jnp.concatenate(
                [jnp.sum(lo * col, axis=0, keepdims=True) for lo, _ in halves]
                + [jnp.sum(hi * col, axis=0, keepdims=True) for _, hi in halves], axis=1))
        o_ref[pl.ds(pl.multiple_of(g * SUBLANES, SUBLANES), SUBLANES), :] = jnp.concatenate(rows, axis=0)
        return carry

    lax.fori_loop(0, tt // SUBLANES, group, 0)


def _tc_table_specs(tab4, hk, tt):
    lines_total = tab4.shape[0]
    return [pl.BlockSpec((lines_total, LANES), lambda i: (0, 0), pipeline_mode=pl.Buffered(1)),
            pl.BlockSpec((tt, hk), lambda i: (i, 0), memory_space=pltpu.SMEM)]


def _tc_udot(tab4, idx, x):
    t, d = x.shape
    hk = idx.shape[1]
    tt = min(TC_RETRIEVE_TILE, t)
    return _ordered_call(
        _tc_udot_body, grid=(t // tt,),
        in_specs=_tc_table_specs(tab4, hk, tt) + [pl.BlockSpec((tt, d), lambda i: (i, 0))],
        out_specs=pl.BlockSpec((hk, tt), lambda i: (0, i)),
        out_shape=jax.ShapeDtypeStruct((hk, t), F32),
        scratch_shapes=[pltpu.VMEM((d // 2 // LANES * TC_ROW_STRIDE, LANES), jnp.int32)] * 2,
        compiler_params=_params("arbitrary"), name="tc_udot",
    )(tab4, idx, x)


def _tc_vsum(tab4, idx, w_t):
    hk, t = w_t.shape
    d = tab4.shape[1] * 2 * (tab4.shape[0] // PEER_NKEYS ** 2)
    tt = min(TC_RETRIEVE_TILE, t)
    return _ordered_call(
        _tc_vsum_body, grid=(t // tt,),
        in_specs=_tc_table_specs(tab4, hk, tt) + [pl.BlockSpec((hk, tt), lambda i: (0, i))],
        out_specs=pl.BlockSpec((tt, d), lambda i: (i, 0)),
        out_shape=jax.ShapeDtypeStruct((t, d), F32),
        scratch_shapes=[pltpu.VMEM((d // 2 // LANES * TC_ROW_STRIDE, LANES), jnp.int32)] * 2,
        compiler_params=_params("arbitrary"), name="tc_vsum",
    )(tab4, idx, w_t)


def _peer_retrieve_tc(xn, idx_t, gate_t, u_tab4, v_tab4):
    hk, t = idx_t.shape
    idx = idx_t.T * (u_tab4.shape[0] // PEER_NKEYS ** 2)
    a_t = _tc_udot(u_tab4, idx, xn)

    def finish():
        tt = min(TC_RETRIEVE_TILE, t)
        blk = pl.BlockSpec((hk, tt), lambda i: (0, i))
        w_t = _ordered_call(
            _gate_body, grid=(t // tt,), in_specs=[blk, blk], out_specs=blk,
            out_shape=jax.ShapeDtypeStruct((hk, t), F32), compiler_params=_params("arbitrary"), name="tc_gate",
        )(a_t, gate_t)
        return _tc_vsum(v_tab4, idx, w_t)

    return finish


def _peer_start(x, g, w_q, keys, tables, on_tc):
    bsz, s_len, d = x.shape
    xn, idx_t, gate_t = _peer_route(x.reshape(bsz * s_len, d), g, w_q, keys)
    if on_tc:
        fin = _peer_retrieve_tc(xn, idx_t, gate_t, tables[2], tables[3])
    else:
        fin = _peer_retrieve(xn, idx_t.T, gate_t.T, tables[0], tables[1])
    return lambda: fin().reshape(bsz, s_len, d)


def _kv_body(fox_w, x_ref, po_ref, g_ref, w_ref, bf_ref, k_ref, v_ref, c_ref, ct_ref, carry):
    x = x_ref[0] + po_ref[0]
    tm = x.shape[0]
    hs = _rms(x, g_ref[...]).astype(BF16)
    z = jnp.dot(hs, w_ref[...], preferred_element_type=F32)
    k_ref[0] = z[:, :fox_w].astype(BF16)
    v_ref[0] = z[:, fox_w:2 * fox_w].astype(BF16)
    zf = z[:, 2 * fox_w:] + bf_ref[...]
    logf = jnp.minimum(zf, 0.0) - jnp.log1p(jnp.exp(-jnp.abs(zf)))

    @pl.when(pl.program_id(1) == 0)
    def _():
        carry[...] = jnp.zeros(carry.shape, F32)

    r = lax.broadcasted_iota(jnp.int32, (tm, tm), 0)
    c = lax.broadcasted_iota(jnp.int32, (tm, tm), 1)
    tri = (c <= r).astype(F32)
    cs = jnp.dot(tri, logf, preferred_element_type=F32, precision=lax.Precision.HIGHEST) + carry[...]
    carry[...] = cs[tm - 1:tm, :]
    c_ref[0] = cs
    ct_ref[0] = cs.T[0:ct_ref.shape[1], :]


def _kv_proj(x, po, g, w_kv, b_f):
    bsz, s_len, d = x.shape
    tm = min(TOKEN_TILE, s_len)
    n_heads = b_f.shape[0]
    fox_w = (w_kv.shape[1] - n_heads) // 2
    w = jnp.pad(w_kv, ((0, 0), (0, LANES - n_heads))).astype(BF16)
    bf = jnp.pad(b_f, (0, LANES - n_heads)).reshape(1, LANES)
    hp = -(-n_heads // SUBLANES) * SUBLANES
    tok = lambda w_, dt: (pl.BlockSpec((1, tm, w_), lambda b, i: (b, i, 0)), jax.ShapeDtypeStruct((bsz, s_len, w_), dt))
    (ks, ksh), (vs, vsh), (cs, csh) = tok(fox_w, BF16), tok(fox_w, BF16), tok(LANES, F32)
    return _ordered_call(
        functools.partial(_kv_body, fox_w),
        grid=(bsz, s_len // tm),
        in_specs=[
            pl.BlockSpec((1, tm, d), lambda b, i: (b, i, 0)),
            pl.BlockSpec((1, tm, d), lambda b, i: (b, i, 0)),
            pl.BlockSpec((1, d), lambda b, i: (0, 0)),
            pl.BlockSpec(w.shape, lambda b, i: (0, 0)),
            pl.BlockSpec((1, LANES), lambda b, i: (0, 0)),
        ],
        out_specs=[ks, vs, cs, pl.BlockSpec((1, hp, tm), lambda b, i: (b, 0, i))],
        out_shape=[ksh, vsh, csh, jax.ShapeDtypeStruct((bsz, hp, s_len), F32)],
        scratch_shapes=[pltpu.VMEM((1, LANES), F32)],
        compiler_params=_params("arbitrary", "arbitrary"),
        name="kv_proj",
    )(x, po, g.reshape(1, d), w, bf)


def _mix_b_in_body(fox_w, n_mem_heads, n_mem, x_ref, po_ref, g_ref, win_ref, mk_ref, mv_ref, q_ref, ym_ref):
    x = x_ref[0] + po_ref[0]
    h = _rms(x, g_ref[...]).astype(BF16)
    z = jnp.dot(h, win_ref[...], preferred_element_type=F32)
    q_ref[0] = (z[:, :fox_w] * (HEAD_DIM ** -0.5)).astype(BF16)
    ym_ref[0] = _mem_attend(z[:, fox_w:], mk_ref[0], mv_ref[0], n_mem_heads, n_mem).astype(BF16)


def _mix_b_in(x, po, g, w_in, mkbd, mvbd):
    bsz, s_len, d = x.shape
    tm = min(TOKEN_TILE, s_len)
    mem_w = mvbd.shape[2]
    fox_w = w_in.shape[1] - mem_w
    n_mem_heads = mem_w // HEAD_DIM
    n_mem = mkbd.shape[2] // n_mem_heads
    win = w_in.astype(BF16)
    tok = lambda w_: pl.BlockSpec((1, tm, w_), lambda b, i: (b, i, 0))
    per_b = lambda a: pl.BlockSpec((1,) + a.shape[1:], lambda b, i: (b, 0, 0))
    return _ordered_call(
        functools.partial(_mix_b_in_body, fox_w, n_mem_heads, n_mem),
        grid=(bsz, s_len // tm),
        in_specs=[tok(d), tok(d), pl.BlockSpec((1, d), lambda b, i: (0, 0)),
                  pl.BlockSpec(win.shape, lambda b, i: (0, 0)), per_b(mkbd), per_b(mvbd)],
        out_specs=[tok(fox_w), tok(mem_w)],
        out_shape=[jax.ShapeDtypeStruct((bsz, s_len, fox_w), BF16),
                   jax.ShapeDtypeStruct((bsz, s_len, mem_w), BF16)],
        compiler_params=_params("arbitrary", "arbitrary"),
        name="mix_b_in",
    )(x, po, g.reshape(1, d), win, mkbd, mvbd)


def _fox_body(n_heads, q_ref, k_ref, v_ref, c_ref, ct_ref, o_ref, m_scr, l_scr, cq_scr, acc):
    qi, ki = pl.program_id(1), pl.program_id(2)
    tq, tk = q_ref.shape[1], k_ref.shape[1]

    def wide(a):
        return jnp.concatenate([a] * (tk // LANES), axis=1)

    @pl.when(ki == 0)
    def _():
        m_scr[...] = jnp.full(m_scr.shape, -jnp.inf, F32)
        l_scr[...] = jnp.zeros(l_scr.shape, F32)
        acc[...] = jnp.zeros(acc.shape, F32)
        cq = c_ref[0]
        for h in range(n_heads):
            cq_scr[h] = jnp.broadcast_to(cq[:, h:h + 1], (tq, LANES))

    def step(masked):
        if masked:
            keep = (lax.broadcasted_iota(jnp.int32, (tq, tk), 0)
                    >= lax.broadcasted_iota(jnp.int32, (tq, tk), 1))
        for h in range(n_heads):
            sl = slice(h * HEAD_DIM, (h + 1) * HEAD_DIM)
            s = lax.dot_general(q_ref[0, :, sl], k_ref[0, :, sl], (((1,), (1,)), ((), ())),
                                preferred_element_type=F32)
            s = s + (wide(cq_scr[h]) - ct_ref[0, h:h + 1, :])
            if masked:
                s = jnp.where(keep, s, -jnp.inf)
            m_old = m_scr[h]
            m_new = jnp.maximum(m_old, jnp.max(s, axis=-1, keepdims=True))
            alpha = jnp.exp(m_old - m_new)
            p = jnp.exp(s - wide(m_new))
            l_scr[h] = alpha * l_scr[h] + jnp.sum(p, axis=-1, keepdims=True)
            acc[h] = alpha[:, :HEAD_DIM] * acc[h] + jnp.dot(p.astype(BF16), v_ref[0, :, sl],
                                                           preferred_element_type=F32)
            m_scr[h] = m_new

    @pl.when(ki < qi)
    def _():
        step(False)

    @pl.when(ki == qi)
    def _():
        step(True)
        for h in range(n_heads):
            sl = slice(h * HEAD_DIM, (h + 1) * HEAD_DIM)
            o_ref[0, :, sl] = (acc[h] / l_scr[h][:, :HEAD_DIM]).astype(BF16)


def _fox(q, k, v, c, ct):
    bsz, s_len, w = q.shape
    n_heads = w // HEAD_DIM
    t = min(ATTN_TILE, s_len)
    nb = s_len // t
    return _ordered_call(
        functools.partial(_fox_body, n_heads),
        grid=(bsz, nb, nb),
        in_specs=[
            pl.BlockSpec((1, t, w), lambda b, i, j: (b, i, 0)),
            pl.BlockSpec((1, t, w), lambda b, i, j: (b, jnp.minimum(i, j), 0)),
            pl.BlockSpec((1, t, w), lambda b, i, j: (b, jnp.minimum(i, j), 0)),
            pl.BlockSpec((1, t, c.shape[2]), lambda b, i, j: (b, i, 0)),
            pl.BlockSpec((1, ct.shape[1], t), lambda b, i, j: (b, 0, jnp.minimum(i, j))),
        ],
        out_specs=pl.BlockSpec((1, t, w), lambda b, i, j: (b, i, 0)),
        out_shape=jax.ShapeDtypeStruct((bsz, s_len, w), BF16),
        scratch_shapes=[pltpu.VMEM((n_heads, t, LANES), F32), pltpu.VMEM((n_heads, t, LANES), F32),
                        pltpu.VMEM((n_heads, t, LANES), F32), pltpu.VMEM((n_heads, t, HEAD_DIM), F32)],
        compiler_params=_params("arbitrary", "arbitrary", "arbitrary"),
        name="fox",
    )(q, k, v, c, ct)


def _mix_b_out_body(fox_w, x_ref, po_ref, y_ref, ym_ref, wout_ref, o_ref):
    o_ref[0] = (x_ref[0] + po_ref[0]
                + jnp.dot(y_ref[0], wout_ref[0:fox_w, :], preferred_element_type=F32)
                + jnp.dot(ym_ref[0], wout_ref[fox_w:, :], preferred_element_type=F32))


def _mix_b_out(x, po, y, ym, w_out):
    bsz, s_len, d = x.shape
    tm = min(TOKEN_TILE, s_len)
    fox_w = y.shape[2]
    wout = w_out.astype(BF16)
    tok = lambda w_: pl.BlockSpec((1, tm, w_), lambda b, i: (b, i, 0))
    return _ordered_call(
        functools.partial(_mix_b_out_body, fox_w),
        grid=(bsz, s_len // tm),
        in_specs=[tok(d), tok(d), tok(fox_w), tok(ym.shape[2]), pl.BlockSpec(wout.shape, lambda b, i: (0, 0))],
        out_specs=tok(d),
        out_shape=jax.ShapeDtypeStruct(x.shape, F32),
        compiler_params=_params("arbitrary", "arbitrary"),
        name="mix_b_out",
    )(x, po, y, ym, wout)


def _final_body(x_ref, po_ref, g_ref, o_ref):
    o_ref[0] = _rms(x_ref[0] + po_ref[0], g_ref[...])


def _final_norm(x, po, g):
    bsz, s_len, d = x.shape
    tm = min(TOKEN_TILE, s_len)
    tok = pl.BlockSpec((1, tm, d), lambda b, i: (b, i, 0))
    return _ordered_call(
        _final_body,
        grid=(bsz, s_len // tm),
        in_specs=[tok, tok, pl.BlockSpec((1, d), lambda b, i: (0, 0))],
        out_specs=tok,
        out_shape=jax.ShapeDtypeStruct(x.shape, F32),
        compiler_params=_params("arbitrary", "arbitrary"),
        name="final_norm",
    )(x, po, g.reshape(1, d))


def kernel(x, mem, g_mix, w_in_a, conv_w, w_in_b, w_out, g_mem, w_mem_kv, g_kv, w_kv, b_f, g_ffn,
           peer_wq, peer_keys, peer_u, peer_v, g_final):
    depth = g_mix.shape[0]
    n_a = w_in_a.shape[0]
    n_mem_heads = w_mem_kv.shape[2] // (2 * HEAD_DIM)
    mkbd, mvbd = _block_diag_mem(_memkv(mem, g_mem, w_mem_kv), n_mem_heads)

    bsz = x.shape[0]
    eighths = sum(r for r, _ in BATCH_STREAMS)
    plan, unit = (BATCH_STREAMS, bsz // eighths) if bsz % eighths == 0 else (((1, "s"),), bsz)
    tables = []
    for l in range(depth):
        u_pk, v_pk = _pack_rows(peer_u, l), _pack_rows(peer_v, l)
        tables.append((u_pk, v_pk, _pack_rows_tc(u_pk), _pack_rows_tc(v_pk)))
    states, lo = [], 0
    for r, letters in plan:
        rows = r * unit
        states.append(dict(x=x[lo:lo + rows], po=None, lo=lo, hi=lo + rows,
                           units=[letters[min(l, len(letters) - 1)] == "t" for l in range(depth + 1)]))
        lo += rows
    _issue.last = None

    def front(st, layer):
        xs, po = st["x"], st["po"]
        mk, mv = mkbd[layer, st["lo"]:st["hi"]], mvbd[layer, st["lo"]:st["hi"]]
        if layer == n_a:
            st["kv"] = _kv_proj(xs, po, g_kv, w_kv, b_f)
        if layer < n_a and layer == 0:
            xs = _mix_a(x, None, g_mix[0], w_in_a[0], conv_w[0], mk, mv, w_out[0], row0=st["lo"])
        elif layer < n_a:
            xs = _mix_a(xs, po, g_mix[layer], w_in_a[layer], conv_w[layer], mk, mv, w_out[layer])
        else:
            q, ym = _mix_b_in(xs, po, g_mix[layer], w_in_b[layer - n_a], mk, mv)
            y = _fox(q, *st["kv"])
            xs = _mix_b_out(xs, po, y, ym, w_out[layer])
        st["x"] = xs
        st["fin"] = _peer_start(xs, g_ffn[layer], peer_wq[layer], peer_keys[layer], tables[layer],
                                st["units"][layer])

    def back(st):
        st["po"] = st["fin"]()

    for st in states:
        st.update(layer=0, next="front", ready=0.0)
    tc_clock = sc_clock = 0.0
    while any(st["layer"] < depth for st in states):
        live = [st for st in states if st["layer"] < depth]
        st = min(live, key=lambda s: (
            max(s["ready"], tc_clock),
            not (s["next"] == "back" and not s["on_tc"]),
            s["on_tc"] if s["next"] == "back" else s["units"][s["layer"]]))
        layer, rows = st["layer"], (st["hi"] - st["lo"]) / unit
        tc_clock = max(tc_clock, st["ready"])
        if st["next"] == "front":
            st["on_tc"] = st["units"][layer]
            front(st, layer)
            tc_clock += rows * (COST_ROUTE + (COST_MIX_ATTN if layer >= n_a else COST_MIX_CONV)
                                + (COST_TC_UDOT if st["on_tc"] else 0.0))
            st["next"] = "back"
            if st["on_tc"]:
                st["ready"] = tc_clock
            else:
                sc_clock = max(sc_clock, tc_clock) + rows * COST_SC_PASS
                st["ready"] = sc_clock
        else:
            back(st)
            if st["on_tc"]:
                tc_clock += rows * COST_TC_BACK
                st["ready"] = tc_clock
            else:
                sc_clock = max(sc_clock, tc_clock) + rows * COST_SC_PASS
                st["ready"] = sc_clock
            st["next"], st["layer"] = "front", layer + 1
    outs = [_final_norm(st["x"], st["po"], g_final) for st in states]
    _issue.last = None
    return jnp.concatenate(outs, axis=0)
```
